```python
import math
import jax, jax.numpy as jnp
from jax import lax
import numpy as np

D_MODEL = 1024
BATCH = 16
SEQ = 2048
DEPTH = 1

MOBA_HEADS = 8
MOBA_HEAD_DIM = 64
MOBA_BLOCK = 256
MOBA_TOPK = 3
MOBA_Q_CHUNK = 16
MOBA_WIDTH = MOBA_HEADS * MOBA_HEAD_DIM
DIFF_HEADS = 4
DIFF_HEAD_DIM = 64
DIFF_Q_BLOCK = 128
DIFF_WIDTH = DIFF_HEADS * 2 * DIFF_HEAD_DIM
N_BRANCHES = 2
IN_COLS = 3 * MOBA_WIDTH + 3 * DIFF_WIDTH + N_BRANCHES * D_MODEL
D_FF = -(-8 * D_MODEL // (3 * 256)) * 256
NORM_EPS = 1e-6
SUBLN_EPS = 1e-5
ALIBI_MAX_BIAS = 8.0
NEG = -1e30

kernel_name = 'hybrid_moba_diffattn_gated_sandwich_block'


def rms_norm(x, g, eps=NORM_EPS):
    x32 = x.astype(jnp.float32)
    y = x32 * lax.rsqrt(jnp.mean(x32 * x32, axis=-1, keepdims=True) + eps)
    return (y * g.astype(jnp.float32)).astype(x.dtype)


def alibi_slopes():
    n = MOBA_HEADS + DIFF_HEADS
    slopes = 2.0 ** (-ALIBI_MAX_BIAS * np.arange(1, n + 1) / n)
    stride = n // DIFF_HEADS
    diff_idx = np.arange(DIFF_HEADS) * stride + (stride - 1)
    moba_idx = np.setdiff1d(np.arange(n), diff_idx)
    return (jnp.asarray(slopes[moba_idx], dtype=jnp.float32),
            jnp.asarray(slopes[diff_idx], dtype=jnp.float32))


def moba_attention(q, k, v, slopes):
    B, H, S, Dh = q.shape
    BS = MOBA_BLOCK
    nb = -(-S // BS)
    s_pad = nb * BS
    scale = Dh ** -0.5
    pad = ((0, 0), (0, 0), (0, s_pad - S), (0, 0))
    k_pad = jnp.pad(k, pad)
    v_pad = jnp.pad(v, pad)
    k_blocks = k_pad.reshape(B, H, nb, BS, Dh)
    v_blocks = v_pad.reshape(B, H, nb, BS, Dh)
    k_mean = jnp.mean(k_blocks.astype(jnp.float32), axis=3).astype(q.dtype)
    pos = jnp.arange(S)
    q_blk = pos // BS
    gate = jnp.einsum('bhtd,bhnd->bhtn', q, k_mean).astype(jnp.float32)
    is_past = jnp.arange(nb)[None, :] < q_blk[:, None]
    gate = jnp.where(is_past, gate, NEG)
    n_sel = max(1, min(MOBA_TOPK, nb - 1))
    _, sel = lax.top_k(gate, n_sel)
    sel_valid = jnp.arange(n_sel)[None, :] < q_blk[:, None]
    b_ix = jnp.arange(B)[:, None, None, None]
    h_ix = jnp.arange(H)[None, :, None, None]
    C = MOBA_Q_CHUNK
    sl = slopes[None, :, None, None, None]

    def chunk(c):
        t0 = c * C
        qc = lax.dynamic_slice_in_dim(q, t0, C, axis=2)
        sc = lax.dynamic_slice_in_dim(sel, t0, C, axis=2)
        vc = lax.dynamic_slice_in_dim(sel_valid, t0, C, axis=0)
        tq = t0 + jnp.arange(C)
        kg = k_blocks[b_ix, h_ix, sc]
        s_pos = sc[..., None] * BS + jnp.arange(BS)
        lp = jnp.einsum('bhcd,bhcnkd->bhcnk', qc, kg).astype(jnp.float32) * scale
        lp = lp - sl * (tq[:, None, None] - s_pos).astype(jnp.float32)
        lp = jnp.where(vc[:, :, None], lp, NEG).reshape(B, H, C, n_sel * BS)
        own0 = (t0 // BS) * BS
        ko = lax.dynamic_slice_in_dim(k_pad, own0, BS, axis=2)
        vo = lax.dynamic_slice_in_dim(v_pad, own0, BS, axis=2)
        so = own0 + jnp.arange(BS)
        dist_o = (tq[:, None] - so[None, :]).astype(jnp.float32)
        lo = jnp.einsum('bhcd,bhkd->bhck', qc, ko).astype(jnp.float32) * scale
        lo = lo - slopes[None, :, None, None] * dist_o
        lo = jnp.where(so[None, :] <= tq[:, None], lo, NEG)
        p = jax.nn.softmax(jnp.concatenate([lp, lo], axis=-1), axis=-1).astype(v.dtype)
        pp = p[..., :n_sel * BS].reshape(B, H, C, n_sel, BS)
        po = p[..., n_sel * BS:]
        vg = v_blocks[b_ix, h_ix, sc]
        return (jnp.einsum('bhcnk,bhcnkd->bhcd', pp, vg)
                + jnp.einsum('bhck,bhkd->bhcd', po, vo))

    outs = lax.map(chunk, jnp.arange(S // C))
    return outs.transpose(1, 2, 0, 3, 4).reshape(B, H, S, Dh)


def diff_attention(q1, q2, k1, k2, v, lam, slopes):
    B, H, S, d = q1.shape
    scale = d ** -0.5
    QB = DIFF_Q_BLOCK
    kpos = jnp.arange(S)

    def block(c):
        t0 = c * QB
        tq = t0 + jnp.arange(QB)
        bias = -slopes[:, None, None] * (tq[:, None] - kpos[None, :]).astype(jnp.float32)
        causal = kpos[None, :] <= tq[:, None]

        def probs(q, k):
            qc = lax.dynamic_slice_in_dim(q, t0, QB, axis=2)
            logits = jnp.einsum('bhqd,bhkd->bhqk', qc, k).astype(jnp.float32) * scale + bias[None]
            return jax.nn.softmax(jnp.where(causal, logits, NEG), axis=-1)

        a = probs(q1, k1) - lam * probs(q2, k2)
        return jnp.einsum('bhqk,bhke->bhqe', a.astype(v.dtype), v)

    outs = lax.map(block, jnp.arange(S // QB))
    return outs.transpose(1, 2, 0, 3, 4).reshape(B, H, S, 2 * d)


def setup_inputs(seed: int = 0) -> dict:
    key = jax.random.key(seed)
    ks = jax.random.split(key, 17)
    L, D = DEPTH, D_MODEL

    def normal(k, shape, scale):
        return jax.random.normal(k, shape, dtype=jnp.float32) * scale

    def gain(k, n):
        return 1.0 + normal(k, (L, n), 0.05)

    return {
        'x': normal(ks[0], (BATCH, SEQ, D), 1.0),
        'norm_mix_pre_g': gain(ks[1], D),
        'w_in': normal(ks[2], (L, D, IN_COLS), D ** -0.5),
        'w_branch_a': normal(ks[3], (L, MOBA_WIDTH, D), MOBA_WIDTH ** -0.5),
        'w_branch_b': normal(ks[4], (L, DIFF_WIDTH, D), DIFF_WIDTH ** -0.5),
        'lam_q1': normal(ks[5], (L, DIFF_HEAD_DIM), 0.1),
        'lam_k1': normal(ks[6], (L, DIFF_HEAD_DIM), 0.1),
        'lam_q2': normal(ks[7], (L, DIFF_HEAD_DIM), 0.1),
        'lam_k2': normal(ks[8], (L, DIFF_HEAD_DIM), 0.1),
        'diff_subln_g': gain(ks[9], 2 * DIFF_HEAD_DIM),
        'w_out': normal(ks[10], (L, D, D), D ** -0.5),
        'norm_mix_post_g': gain(ks[11], D),
        'norm_ffn_pre_g': gain(ks[12], D),
        'w_gate': normal(ks[13], (L, D, D_FF), D ** -0.5),
        'w_up': normal(ks[14], (L, D, D_FF), D ** -0.5),
        'w_down': normal(ks[15], (L, D_FF, D), D_FF ** -0.5),
        'norm_ffn_post_g': gain(ks[16], D),
    }


def reference(x, norm_mix_pre_g, w_in, w_branch_a, w_branch_b, lam_q1, lam_k1, lam_q2, lam_k2,
              diff_subln_g, w_out, norm_mix_post_g, norm_ffn_pre_g, w_gate, w_up, w_down,
              norm_ffn_post_g):
    B, S, D = x.shape
    moba_slopes, diff_slopes = alibi_slopes()
    sizes = [MOBA_WIDTH] * 3 + [DIFF_WIDTH] * 3 + [D_MODEL] * N_BRANCHES
    split_at = [int(v) for v in np.cumsum(sizes)[:-1]]
    for l in range(DEPTH):
        lam_init = 0.8 - 0.6 * math.exp(-0.3 * l)
        h = rms_norm(x, norm_mix_pre_g[l])
        proj = h @ w_in[l]
        mq, mk, mv, dq, dk, dv, ga, gb = jnp.split(proj, split_at, axis=-1)

        def moba_heads(t):
            return t.reshape(B, S, MOBA_HEADS, MOBA_HEAD_DIM).transpose(0, 2, 1, 3)

        ya = moba_attention(moba_heads(mq), moba_heads(mk), moba_heads(mv), moba_slopes)
        ya = ya.transpose(0, 2, 1, 3).reshape(B, S, MOBA_WIDTH)

        dq = dq.reshape(B, S, DIFF_HEADS, 2, DIFF_HEAD_DIM).transpose(0, 2, 3, 1, 4)
        dk = dk.reshape(B, S, DIFF_HEADS, 2, DIFF_HEAD_DIM).transpose(0, 2, 3, 1, 4)
        dv = dv.reshape(B, S, DIFF_HEADS, 2 * DIFF_HEAD_DIM).transpose(0, 2, 1, 3)
        lam = (jnp.exp(jnp.sum(lam_q1[l].astype(jnp.float32) * lam_k1[l].astype(jnp.float32)))
               - jnp.exp(jnp.sum(lam_q2[l].astype(jnp.float32) * lam_k2[l].astype(jnp.float32)))
               + lam_init)
        yb = diff_attention(dq[:, :, 0], dq[:, :, 1], dk[:, :, 0], dk[:, :, 1], dv, lam, diff_slopes)
        yb = rms_norm(yb, diff_subln_g[l], SUBLN_EPS) * (1.0 - lam_init)
        yb = yb.transpose(0, 2, 1, 3).reshape(B, S, DIFF_WIDTH)

        merged = (jax.nn.sigmoid(ga) * (ya @ w_branch_a[l])
                  + jax.nn.sigmoid(gb) * (yb @ w_branch_b[l]))
        x = x + rms_norm(merged @ w_out[l], norm_mix_post_g[l])
        h = rms_norm(x, norm_ffn_pre_g[l])
        f = (jax.nn.silu(h @ w_gate[l]) * (h @ w_up[l])) @ w_down[l]
        x = x + rms_norm(f, norm_ffn_post_g[l])
    return x
```

```python
import functools
import math

import numpy as np
import jax
import jax.numpy as jnp
from jax import lax
from jax.experimental import pallas as pl
from jax.experimental.pallas import tpu as pltpu

D_MODEL = 1024
MOBA_HEADS = 8
MOBA_BLOCK = 256
MOBA_TOPK = 3
MOBA_WIDTH = 512
DIFF_HEADS = 4
DIFF_HEAD_DIM = 64
DIFF_WIDTH = 512
HEAD_DIM = 64
QKV_COLS = 3 * MOBA_WIDTH + 3 * DIFF_WIDTH
D_FF = 2816
NORM_EPS = 1e-6
SUBLN_EPS = 1e-5
ALIBI_MAX_BIAS = 8.0
NEG = -1e30

LANES = 128
BLK = 256
N_COL_TILES = QKV_COLS // LANES
ONES_ROWS = 16
VMEM_LIMIT = 56 * 1024 * 1024

BF16 = jnp.bfloat16
F32 = jnp.float32


def _alibi_slopes():
    n = MOBA_HEADS + DIFF_HEADS
    slopes = 2.0 ** (-ALIBI_MAX_BIAS * np.arange(1, n + 1) / n)
    stride = n // DIFF_HEADS
    diff_idx = np.arange(DIFF_HEADS) * stride + (stride - 1)
    moba_idx = np.setdiff1d(np.arange(n), diff_idx)
    return (np.asarray(slopes[moba_idx], np.float32), np.asarray(slopes[diff_idx], np.float32))


def _dot(a, b):
    return jnp.dot(a, b, preferred_element_type=F32)


def _dot_nt(a, b):
    return lax.dot_general(a, b, (((1,), (1,)), ((), ())), preferred_element_type=F32)


def _rms(x, g, eps):
    return x * lax.rsqrt(jnp.mean(x * x, axis=-1, keepdims=True) + eps) * g


def _in_proj_kernel(x_ref, g_ref, w_ref, o_ref, *, chunk):
    h = _rms(x_ref[0], g_ref[...], NORM_EPS).astype(BF16)
    tiles_per_chunk = chunk // LANES
    for c in range(QKV_COLS // chunk):
        acc = _dot(h, w_ref[:, c * chunk:(c + 1) * chunk])
        for t in range(tiles_per_chunk):
            tile = c * tiles_per_chunk + t
            val = acc[:, t * LANES:(t + 1) * LANES]
            if tile < 4 or 12 <= tile < 16:
                val = val * (HEAD_DIM ** -0.5)
            o_ref[0, tile] = val.astype(BF16)


def _in_proj(x, g, w_qkv, tm=512):
    B, S, D = x.shape
    return pl.pallas_call(
        functools.partial(_in_proj_kernel, chunk=512),
        grid=(B, S // tm),
        in_specs=[
            pl.BlockSpec((1, tm, D), lambda b, i: (b, i, 0)),
            pl.BlockSpec((1, D), lambda b, i: (0, 0)),
            pl.BlockSpec((D, QKV_COLS), lambda b, i: (0, 0)),
        ],
        out_specs=pl.BlockSpec((1, N_COL_TILES, tm, LANES), lambda b, i: (b, 0, i, 0)),
        out_shape=jax.ShapeDtypeStruct((B, N_COL_TILES, S, LANES), BF16),
        compiler_params=pltpu.CompilerParams(
            dimension_semantics=("arbitrary", "arbitrary"), vmem_limit_bytes=VMEM_LIMIT),
        name="in_proj",
    )(x, g.reshape(1, D), w_qkv)


def _attend_t(qj, k_block, vt_block, j, diag_add, past_add, row_term):
    sd = _dot_nt(k_block(j), qj) + diag_add
    m = jnp.max(sd, axis=0, keepdims=True)
    p = jnp.exp(sd - m)
    acc = _dot(vt_block(j), p.astype(BF16))

    def body(n, carry):
        m, acc = carry
        s = _dot_nt(k_block(n), qj) + past_add + row_term(n)
        m_new = jnp.maximum(m, jnp.max(s, axis=0, keepdims=True))
        alpha = jnp.exp(m - m_new)
        p = jnp.exp(s - m_new)
        acc = alpha * acc + _dot(vt_block(n), p.astype(BF16))
        return m_new, acc

    if j > 0:
        m, acc = lax.fori_loop(0, j, body, (m, acc))
    return acc


def _bias_tiles(slope):
    rk = lax.broadcasted_iota(jnp.int32, (BLK, BLK), 0)
    rq = lax.broadcasted_iota(jnp.int32, (BLK, BLK), 1)
    past_add = slope * rk.astype(F32)
    diag_add = jnp.where(rk <= rq, past_add, NEG)
    return diag_add, past_add


def _store_vt(v2, vt_ref, rows_of):
    S = v2.shape[0]
    vt = v2.astype(F32).T
    ones = jnp.ones((ONES_ROWS, S), F32)
    for idx, (lo, hi) in enumerate(rows_of):
        aug = jnp.concatenate([vt[lo:hi], ones], axis=0).astype(BF16)
        for n in range(S // BLK):
            vt_ref[idx, n] = aug[:, n * BLK:(n + 1) * BLK]


def _moba_kernel(slopes_ref, q_ref, k_ref, v_ref, o_ref, kh_ref, vt_ref, pen_ref):
    S = q_ref.shape[2]
    nb = S // BLK
    hp = pl.program_id(1)
    lane = lax.broadcasted_iota(jnp.int32, (1, LANES), 1)
    row8 = lax.broadcasted_iota(jnp.int32, (nb, BLK), 0)

    kf = k_ref[0, 0].astype(F32)
    gates = []
    for hh in range(2):
        in_head = (lane >= hh * HEAD_DIM) & (lane < (hh + 1) * HEAD_DIM)
        khf = jnp.where(in_head, kf, 0.0)
        kh = khf.astype(BF16)
        for n in range(nb):
            kh_ref[hh, n] = kh[n * BLK:(n + 1) * BLK]
        kmean = jnp.sum(khf.reshape(nb, BLK, LANES), axis=1) * (1.0 / BLK)
        km_hi = kmean.astype(BF16)
        km_lo = (kmean - km_hi.astype(F32)).astype(BF16)
        g2 = _dot_nt(jnp.concatenate([km_hi, km_lo], axis=0), q_ref[0, 0])
        gates.append(g2[:nb] + g2[nb:])
    _store_vt(v_ref[0, 0], vt_ref, [(0, HEAD_DIM), (HEAD_DIM, 2 * HEAD_DIM)])

    n_sel = max(1, min(MOBA_TOPK, nb - 1))
    for j in range(nb):
        qj = q_ref[0, 0, j * BLK:(j + 1) * BLK, :]
        halves = []
        for hh in range(2):
            slope = slopes_ref[hp * 2 + hh]
            diag_add, past_add = _bias_tiles(slope)
            if j > 0:
                gj = gates[hh][:, j * BLK:(j + 1) * BLK]
                cnt = jnp.zeros((nb, BLK), F32)
                for m_ in range(j):
                    gm = gj[m_:m_ + 1, :]
                    beats = (gm > gj) | ((gm == gj) & (m_ < row8))
                    cnt = cnt + beats.astype(F32)
                sel = (row8 < j) & (cnt < n_sel)
                pen = jnp.where(sel, 0.0, NEG) + slope * ((row8 - j) * BLK).astype(F32)
                pen_ref[...] = pen
            acc = _attend_t(
                qj,
                lambda n, hh=hh: kh_ref[hh, n],
                lambda n, hh=hh: vt_ref[hh, n],
                j, diag_add, past_add,
                lambda n: pen_ref[pl.ds(n, 1), :])
            halves.append(acc[:HEAD_DIM] / acc[HEAD_DIM:HEAD_DIM + 1])
        o_t = jnp.concatenate(halves, axis=0)
        o_ref[0, j * BLK:(j + 1) * BLK, :] = o_t.T.astype(BF16)


def _moba(qkv, slopes):
    B, _, S, _ = qkv.shape
    nb = S // BLK
    n_pairs = MOBA_HEADS // 2
    blk = lambda off: pl.BlockSpec((1, 1, S, LANES), lambda b, h: (b, off + h, 0, 0))
    return pl.pallas_call(
        _moba_kernel,
        grid=(B, n_pairs),
        in_specs=[
            pl.BlockSpec(memory_space=pltpu.SMEM),
            blk(0), blk(n_pairs), blk(2 * n_pairs),
        ],
        out_specs=pl.BlockSpec((1, S, LANES), lambda b, h: (b, 0, h)),
        out_shape=jax.ShapeDtypeStruct((B, S, MOBA_WIDTH), BF16),
        scratch_shapes=[
            pltpu.VMEM((2, nb, BLK, LANES), BF16),
            pltpu.VMEM((2, nb, HEAD_DIM + ONES_ROWS, BLK), BF16),
            pltpu.VMEM((nb, BLK), F32),
        ],
        compiler_params=pltpu.CompilerParams(
            dimension_semantics=("arbitrary", "arbitrary"), vmem_limit_bytes=VMEM_LIMIT),
        name="moba",
    )(slopes, qkv, qkv, qkv)


def _diff_kernel(slopes_ref, q_ref, k_ref, v_ref, lq1_ref, lk1_ref, lq2_ref, lk2_ref, g_ref,
                 o_ref, kh_ref, vt_ref, *, lam_init):
    S = q_ref.shape[2]
    nb = S // BLK
    slope = slopes_ref[pl.program_id(1)]
    lane = lax.broadcasted_iota(jnp.int32, (1, LANES), 1)

    lam = (jnp.exp(jnp.sum(lq1_ref[...] * lk1_ref[...], axis=-1, keepdims=True))
           - jnp.exp(jnp.sum(lq2_ref[...] * lk2_ref[...], axis=-1, keepdims=True))
           + lam_init)

    kf = k_ref[0, 0].astype(F32)
    for mp in range(2):
        in_map = (lane >= mp * DIFF_HEAD_DIM) & (lane < (mp + 1) * DIFF_HEAD_DIM)
        kh = jnp.where(in_map, kf, 0.0).astype(BF16)
        for n in range(nb):
            kh_ref[mp, n] = kh[n * BLK:(n + 1) * BLK]
    _store_vt(v_ref[0, 0], vt_ref, [(0, 2 * DIFF_HEAD_DIM)])

    diag_add, past_add = _bias_tiles(slope)
    dv = 2 * DIFF_HEAD_DIM
    for j in range(nb):
        qj = q_ref[0, 0, j * BLK:(j + 1) * BLK, :]
        outs = []
        for mp in range(2):
            acc = _attend_t(
                qj,
                lambda n, mp=mp: kh_ref[mp, n],
                lambda n: vt_ref[0, n],
                j, diag_add, past_add,
                lambda n, j=j: slope * ((n - j) * BLK).astype(F32))
            outs.append(acc[:dv] / acc[dv:dv + 1])
        a = (outs[0] - lam * outs[1]).T
        y = _rms(a, g_ref[...], SUBLN_EPS) * (1.0 - lam_init)
        o_ref[0, j * BLK:(j + 1) * BLK, :] = y.astype(BF16)


def _diff(qkv, slopes, lq1, lk1, lq2, lk2, subln_g, lam_init):
    B, _, S, _ = qkv.shape
    nb = S // BLK
    H = DIFF_HEADS
    base = 3 * (MOBA_HEADS // 2)
    blk = lambda off: pl.BlockSpec((1, 1, S, LANES), lambda b, h: (b, off + h, 0, 0))
    vec = lambda n: pl.BlockSpec((1, n), lambda b, h: (0, 0))
    d = DIFF_HEAD_DIM
    return pl.pallas_call(
        functools.partial(_diff_kernel, lam_init=lam_init),
        grid=(B, H),
        in_specs=[
            pl.BlockSpec(memory_space=pltpu.SMEM),
            blk(base), blk(base + H), blk(base + 2 * H),
            vec(d), vec(d), vec(d), vec(d), vec(2 * d),
        ],
        out_specs=pl.BlockSpec((1, S, LANES), lambda b, h: (b, 0, h)),
        out_shape=jax.ShapeDtypeStruct((B, S, DIFF_WIDTH), BF16),
        scratch_shapes=[
            pltpu.VMEM((2, nb, BLK, LANES), BF16),
            pltpu.VMEM((1, nb, 2 * d + ONES_ROWS, BLK), BF16),
        ],
        compiler_params=pltpu.CompilerParams(
            dimension_semantics=("arbitrary", "arbitrary"), vmem_limit_bytes=VMEM_LIMIT),
        name="diff_attn",
    )(slopes, qkv, qkv, qkv, lq1.reshape(1, d), lk1.reshape(1, d), lq2.reshape(1, d),
      lk2.reshape(1, d), subln_g.reshape(1, 2 * d))


def _merge_kernel(x_ref, ya_ref, yb_ref, gpre_ref, wg_ref, wa_ref, wb_ref, wo_ref, gpost_ref,
                  o_ref):
    x = x_ref[...]
    D = x.shape[-1]
    h = _rms(x, gpre_ref[...], NORM_EPS).astype(BF16)
    ga = _dot(h, wg_ref[:, :D])
    a = _dot(ya_ref[...], wa_ref[...])
    merged = jax.nn.sigmoid(ga) * a
    gb = _dot(h, wg_ref[:, D:])
    b = _dot(yb_ref[...], wb_ref[...])
    merged = merged + jax.nn.sigmoid(gb) * b
    o = _dot(merged.astype(BF16), wo_ref[...])
    o_ref[...] = x + _rms(o, gpost_ref[...], NORM_EPS)


def _const_spec(shape):
    return pl.BlockSpec(shape, lambda i: (0,) * len(shape), pipeline_mode=pl.Buffered(1))


def _merge(x2, ya2, yb2, gpre, w_g, wa, wb, wo, gpost, tm=512):
    T, D = x2.shape
    return pl.pallas_call(
        _merge_kernel,
        grid=(T // tm,),
        in_specs=[
            pl.BlockSpec((tm, D), lambda i: (i, 0)),
            pl.BlockSpec((tm, MOBA_WIDTH), lambda i: (i, 0)),
            pl.BlockSpec((tm, DIFF_WIDTH), lambda i: (i, 0)),
            _const_spec((1, D)),
            _const_spec((D, 2 * D)),
            _const_spec((MOBA_WIDTH, D)),
            _const_spec((DIFF_WIDTH, D)),
            _const_spec((D, D)),
            _const_spec((1, D)),
        ],
        out_specs=pl.BlockSpec((tm, D), lambda i: (i, 0)),
        out_shape=jax.ShapeDtypeStruct((T, D), F32),
        compiler_params=pltpu.CompilerParams(
            dimension_semantics=("arbitrary",), vmem_limit_bytes=VMEM_LIMIT),
        name="merge",
    )(x2, ya2, yb2, gpre.reshape(1, D), w_g, wa, wb, wo, gpost.reshape(1, D))


def _ffn_kernel(x_ref, gpre_ref, wg_ref, wu_ref, wd_ref, gpost_ref, o_ref, a_ref, *, chunk):
    x = x_ref[...]
    h = _rms(x, gpre_ref[...], NORM_EPS).astype(BF16)
    for c in range(D_FF // chunk):
        cols = slice(c * chunk, (c + 1) * chunk)
        g = _dot(h, wg_ref[:, cols])
        u = _dot(h, wu_ref[:, cols])
        a_ref[:, cols] = (jax.nn.silu(g) * u).astype(BF16)
    f = _dot(a_ref[...], wd_ref[...])
    o_ref[...] = x + _rms(f, gpost_ref[...], NORM_EPS)


def _ffn(x2, gpre, wg, wu, wd, gpost, tm=512):
    T, D = x2.shape
    return pl.pallas_call(
        functools.partial(_ffn_kernel, chunk=D_FF // 2),
        grid=(T // tm,),
        in_specs=[
            pl.BlockSpec((tm, D), lambda i: (i, 0)),
            _const_spec((1, D)),
            _const_spec((D, D_FF)),
            _const_spec((D, D_FF)),
            _const_spec((D_FF, D)),
            _const_spec((1, D)),
        ],
        out_specs=pl.BlockSpec((tm, D), lambda i: (i, 0)),
        out_shape=jax.ShapeDtypeStruct((T, D), F32),
        scratch_shapes=[pltpu.VMEM((tm, D_FF), BF16)],
        compiler_params=pltpu.CompilerParams(
            dimension_semantics=("arbitrary",), vmem_limit_bytes=VMEM_LIMIT),
        name="ffn",
    )(x2, gpre.reshape(1, D), wg, wu, wd, gpost.reshape(1, D))


def kernel(x, norm_mix_pre_g, w_in, w_branch_a, w_branch_b, lam_q1, lam_k1, lam_q2, lam_k2,
           diff_subln_g, w_out, norm_mix_post_g, norm_ffn_pre_g, w_gate, w_up, w_down,
           norm_ffn_post_g):
    B, S, D = x.shape
    assert D == D_MODEL and S % BLK == 0
    moba_slopes, diff_slopes = _alibi_slopes()
    moba_slopes = jnp.asarray(moba_slopes)
    diff_slopes = jnp.asarray(diff_slopes)
    depth = w_in.shape[0]
    for l in range(depth):
        lam_init = 0.8 - 0.6 * math.exp(-0.3 * l)
        w_qkv = w_in[l, :, :QKV_COLS].astype(BF16)
        w_g = w_in[l, :, QKV_COLS:].astype(BF16)
        qkv = _in_proj(x, norm_mix_pre_g[l], w_qkv)
        ya = _moba(qkv, moba_slopes)
        yb = _diff(qkv, diff_slopes, lam_q1[l], lam_k1[l], lam_q2[l], lam_k2[l],
                   diff_subln_g[l], lam_init)
        x2 = _merge(x.reshape(B * S, D), ya.reshape(B * S, MOBA_WIDTH),
                    yb.reshape(B * S, DIFF_WIDTH), norm_mix_pre_g[l], w_g,
                    w_branch_a[l].astype(BF16), w_branch_b[l].astype(BF16),
                    w_out[l].astype(BF16), norm_mix_post_g[l])
        x2 = _ffn(x2, norm_ffn_pre_g[l], w_gate[l].astype(BF16), w_up[l].astype(BF16),
                  w_down[l].astype(BF16), norm_ffn_post_g[l])
        x = x2.reshape(B, S, D)
    return x
```

```python
import functools
import math

import numpy as np
import jax
import jax.numpy as jnp
from jax import lax
from jax.experimental import pallas as pl
from jax.experimental.pallas import tpu as pltpu

D_MODEL = 1024
MOBA_HEADS = 8
MOBA_TOPK = 3
MOBA_WIDTH = 512
DIFF_HEADS = 4
DIFF_HEAD_DIM = 64
DIFF_WIDTH = 512
HEAD_DIM = 64
QKV_COLS = 3 * MOBA_WIDTH + 3 * DIFF_WIDTH
D_FF = 2816
NORM_EPS = 1e-6
SUBLN_EPS = 1e-5
ALIBI_MAX_BIAS = 8.0
NEG = -1e30
LOG2E = math.log2(math.e)
Q_SCALE = HEAD_DIM ** -0.5 * LOG2E

LANES = 128
BLK = 256
N_COL_TILES = QKV_COLS // LANES
ONES_ROWS = 16
EXTRA_PEN = 16
VMEM_LIMIT = 56 * 1024 * 1024

BF16 = jnp.bfloat16
F32 = jnp.float32


def _alibi_slopes():
    n = MOBA_HEADS + DIFF_HEADS
    slopes = 2.0 ** (-ALIBI_MAX_BIAS * np.arange(1, n + 1) / n)
    stride = n // DIFF_HEADS
    diff_idx = np.arange(DIFF_HEADS) * stride + (stride - 1)
    moba_idx = np.setdiff1d(np.arange(n), diff_idx)
    return (np.asarray(slopes[moba_idx], np.float32), np.asarray(slopes[diff_idx], np.float32))


def _split3(v):
    parts, rem = [], float(v)
    for _ in range(3):
        p = float(np.asarray(rem, np.float32).astype(BF16).astype(np.float64))
        parts.append(p)
        rem -= p
    return parts


def _attn_tables(slopes, S):
    nb = S // BLK
    H = len(slopes)
    qx = np.zeros((H, 2, nb, LANES), np.float32)
    kx = np.zeros((2, S, LANES), np.float32)
    pos = np.arange(S)
    for half in range(2):
        e0 = HEAD_DIM * (1 - half)
        for c in range(3):
            kx[half, :, e0 + c] = pos % BLK
            kx[half, :, e0 + 3 + c] = (pos // BLK) * BLK
            kx[half, :, e0 + 6 + c] = 1.0
        for n in range(nb):
            kx[half, n * BLK:(n + 1) * BLK, e0 + EXTRA_PEN + n] = 1.0
        for h in range(H):
            s = float(np.float32(slopes[h])) * LOG2E
            sp = _split3(s)
            for j in range(nb):
                cp = _split3(-s * j * BLK)
                for c in range(3):
                    qx[h, half, j, e0 + c] = sp[c]
                    qx[h, half, j, e0 + 3 + c] = sp[c]
                    qx[h, half, j, e0 + 6 + c] = cp[c]
    return qx, kx


def _dot(a, b):
    return jnp.dot(a, b, preferred_element_type=F32)


def _dot_nt(a, b):
    return lax.dot_general(a, b, (((1,), (1,)), ((), ())), preferred_element_type=F32)


def _rms(x, g, eps):
    return x * lax.rsqrt(jnp.mean(x * x, axis=-1, keepdims=True) + eps) * g


def _in_proj_kernel(x_ref, g_ref, w_ref, o_ref, *, chunk):
    h = _rms(x_ref[0], g_ref[...], NORM_EPS).astype(BF16)
    tiles_per_chunk = chunk // LANES
    for c in range(QKV_COLS // chunk):
        acc = _dot(h, w_ref[:, c * chunk:(c + 1) * chunk])
        for t in range(tiles_per_chunk):
            tile = c * tiles_per_chunk + t
            val = acc[:, t * LANES:(t + 1) * LANES]
            if tile < 4 or 12 <= tile < 16:
                val = val * Q_SCALE
            o_ref[0, tile] = val.astype(BF16)


def _in_proj(x, g, w_qkv, tm=512):
    B, S, D = x.shape
    return pl.pallas_call(
        functools.partial(_in_proj_kernel, chunk=512),
        grid=(B, S // tm),
        in_specs=[
            pl.BlockSpec((1, tm, D), lambda b, i: (b, i, 0)),
            pl.BlockSpec((1, D), lambda b, i: (0, 0)),
            pl.BlockSpec((D, QKV_COLS), lambda b, i: (0, 0)),
        ],
        out_specs=pl.BlockSpec((1, N_COL_TILES, tm, LANES), lambda b, i: (b, 0, i, 0)),
        out_shape=jax.ShapeDtypeStruct((B, N_COL_TILES, S, LANES), BF16),
        compiler_params=pltpu.CompilerParams(
            dimension_semantics=("arbitrary", "arbitrary"), vmem_limit_bytes=VMEM_LIMIT),
        name="in_proj",
    )(x, g.reshape(1, D), w_qkv)


def _causal_tile():
    rk = lax.broadcasted_iota(jnp.int32, (BLK, BLK), 0)
    rq = lax.broadcasted_iota(jnp.int32, (BLK, BLK), 1)
    return jnp.where(rk <= rq, 0.0, NEG)


def _attend_t(qa, ka_ref, vt_ref, idx, vidx, j, causal):
    lo, hi = j * BLK, (j + 1) * BLK
    sd = _dot_nt(ka_ref[idx, lo:hi, :], qa) + causal
    m = jnp.max(sd, axis=0, keepdims=True)
    if j > 0:
        sp = _dot_nt(ka_ref[idx, :lo, :], qa)
        m = jnp.maximum(m, jnp.max(sp, axis=0, keepdims=True))
    acc = _dot(vt_ref[vidx, :, lo:hi], jnp.exp2(sd - m).astype(BF16))
    if j > 0:
        acc = acc + _dot(vt_ref[vidx, :, :lo], jnp.exp2(sp - m).astype(BF16))
    return acc


def _store_vt(v2, vt_ref, rows_of):
    S = v2.shape[0]
    vt = v2.astype(F32).T
    ones = jnp.ones((ONES_ROWS, S), F32)
    for idx, (lo, hi) in enumerate(rows_of):
        vt_ref[idx] = jnp.concatenate([vt[lo:hi], ones], axis=0).astype(BF16)


def _moba_kernel(q_ref, k_ref, v_ref, qx_ref, kx_ref, o_ref, ka_ref, vt_ref):
    S = q_ref.shape[2]
    nb = S // BLK
    lane = lax.broadcasted_iota(jnp.int32, (1, LANES), 1)
    row8 = lax.broadcasted_iota(jnp.int32, (nb, BLK), 0)
    causal = _causal_tile()

    kf = k_ref[0, 0].astype(F32)
    in_head, gates = [], []
    for hh in range(2):
        in_head.append((lane >= hh * HEAD_DIM) & (lane < (hh + 1) * HEAD_DIM))
        ka_ref[hh] = jnp.where(in_head[hh], kf, kx_ref[hh].astype(F32)).astype(BF16)
        khf = jnp.where(in_head[hh], kf, 0.0)
        kmean = jnp.sum(khf.reshape(nb, BLK, LANES), axis=1) * (1.0 / BLK)
        km_hi = kmean.astype(BF16)
        km_lo = (kmean - km_hi.astype(F32)).astype(BF16)
        g2 = _dot_nt(jnp.concatenate([km_hi, km_lo], axis=0), q_ref[0, 0])
        gates.append(g2[:nb] + g2[nb:])
    _store_vt(v_ref[0, 0], vt_ref, [(0, HEAD_DIM), (HEAD_DIM, 2 * HEAD_DIM)])

    n_sel = max(1, min(MOBA_TOPK, nb - 1))
    for j in range(nb):
        qjf = q_ref[0, 0, j * BLK:(j + 1) * BLK, :].astype(F32)
        halves = []
        for hh in range(2):
            ext = qx_ref[0, hh, j:j + 1, :]
            if j > 0:
                gj = gates[hh][:, j * BLK:(j + 1) * BLK]
                cnt = jnp.zeros((nb, BLK), F32)
                for m_ in range(j):
                    gm = gj[m_:m_ + 1, :]
                    beats = (gm > gj) | ((gm == gj) & (m_ < row8))
                    cnt = cnt + beats.astype(F32)
                pen = jnp.where((row8 >= j) | (cnt < n_sel), 0.0, NEG)
                off = HEAD_DIM * (1 - hh) + EXTRA_PEN
                pen_rows = jnp.concatenate(
                    [jnp.zeros((off, BLK), F32), pen, jnp.zeros((LANES - off - nb, BLK), F32)],
                    axis=0)
                ext = ext + pen_rows.T
            qa = jnp.where(in_head[hh], qjf, ext).astype(BF16)
            acc = _attend_t(qa, ka_ref, vt_ref, hh, hh, j, causal)
            halves.append(acc[:HEAD_DIM] / acc[HEAD_DIM:HEAD_DIM + 1])
        o_t = jnp.concatenate(halves, axis=0)
        o_ref[0, j * BLK:(j + 1) * BLK, :] = o_t.T.astype(BF16)


def _moba(qkv, qx, kx):
    B, _, S, _ = qkv.shape
    nb = S // BLK
    n_pairs = MOBA_HEADS // 2
    blk = lambda off: pl.BlockSpec((1, 1, S, LANES), lambda b, h: (b, off + h, 0, 0))
    return pl.pallas_call(
        _moba_kernel,
        grid=(B, n_pairs),
        in_specs=[
            blk(0), blk(n_pairs), blk(2 * n_pairs),
            pl.BlockSpec((1, 2, nb, LANES), lambda b, h: (h, 0, 0, 0)),
            pl.BlockSpec((2, S, LANES), lambda b, h: (0, 0, 0)),
        ],
        out_specs=pl.BlockSpec((1, S, LANES), lambda b, h: (b, 0, h)),
        out_shape=jax.ShapeDtypeStruct((B, S, MOBA_WIDTH), BF16),
        scratch_shapes=[
            pltpu.VMEM((2, S, LANES), BF16),
            pltpu.VMEM((2, HEAD_DIM + ONES_ROWS, S), BF16),
        ],
        compiler_params=pltpu.CompilerParams(
            dimension_semantics=("arbitrary", "arbitrary"), vmem_limit_bytes=VMEM_LIMIT),
        name="moba",
    )(qkv, qkv, qkv, qx, kx)


def _diff_kernel(q_ref, k_ref, v_ref, qx_ref, kx_ref, lq1_ref, lk1_ref, lq2_ref, lk2_ref, g_ref,
                 o_ref, ka_ref, vt_ref, *, lam_init):
    S = q_ref.shape[2]
    nb = S // BLK
    lane = lax.broadcasted_iota(jnp.int32, (1, LANES), 1)
    causal = _causal_tile()

    lam = (jnp.exp(jnp.sum(lq1_ref[...] * lk1_ref[...], axis=-1, keepdims=True))
           - jnp.exp(jnp.sum(lq2_ref[...] * lk2_ref[...], axis=-1, keepdims=True))
           + lam_init)

    kf = k_ref[0, 0].astype(F32)
    in_map = []
    for mp in range(2):
        in_map.append((lane >= mp * DIFF_HEAD_DIM) & (lane < (mp + 1) * DIFF_HEAD_DIM))
        ka_ref[mp] = jnp.where(in_map[mp], kf, kx_ref[mp].astype(F32)).astype(BF16)
    _store_vt(v_ref[0, 0], vt_ref, [(0, 2 * DIFF_HEAD_DIM)])

    dv = 2 * DIFF_HEAD_DIM
    for j in range(nb):
        qjf = q_ref[0, 0, j * BLK:(j + 1) * BLK, :].astype(F32)
        outs = []
        for mp in range(2):
            qa = jnp.where(in_map[mp], qjf, qx_ref[0, mp, j:j + 1, :]).astype(BF16)
            acc = _attend_t(qa, ka_ref, vt_ref, mp, 0, j, causal)
            outs.append(acc[:dv] / acc[dv:dv + 1])
        a = (outs[0] - lam * outs[1]).T
        y = _rms(a, g_ref[...], SUBLN_EPS) * (1.0 - lam_init)
        o_ref[0, j * BLK:(j + 1) * BLK, :] = y.astype(BF16)


def _diff(qkv, qx, kx, lq1, lk1, lq2, lk2, subln_g, lam_init):
    B, _, S, _ = qkv.shape
    nb = S // BLK
    H = DIFF_HEADS
    base = 3 * (MOBA_HEADS // 2)
    blk = lambda off: pl.BlockSpec((1, 1, S, LANES), lambda b, h: (b, off + h, 0, 0))
    vec = lambda n: pl.BlockSpec((1, n), lambda b, h: (0, 0))
    d = DIFF_HEAD_DIM
    return pl.pallas_call(
        functools.partial(_diff_kernel, lam_init=lam_init),
        grid=(B, H),
        in_specs=[
            blk(base), blk(base + H), blk(base + 2 * H),
            pl.BlockSpec((1, 2, nb, LANES), lambda b, h: (h, 0, 0, 0)),
            pl.BlockSpec((2, S, LANES), lambda b, h: (0, 0, 0)),
            vec(d), vec(d), vec(d), vec(d), vec(2 * d),
        ],
        out_specs=pl.BlockSpec((1, S, LANES), lambda b, h: (b, 0, h)),
        out_shape=jax.ShapeDtypeStruct((B, S, DIFF_WIDTH), BF16),
        scratch_shapes=[
            pltpu.VMEM((2, S, LANES), BF16),
            pltpu.VMEM((1, 2 * d + ONES_ROWS, S), BF16),
        ],
        compiler_params=pltpu.CompilerParams(
            dimension_semantics=("arbitrary", "arbitrary"), vmem_limit_bytes=VMEM_LIMIT),
        name="diff_attn",
    )(qkv, qkv, qkv, qx, kx, lq1.reshape(1, d), lk1.reshape(1, d), lq2.reshape(1, d),
      lk2.reshape(1, d), subln_g.reshape(1, 2 * d))


def _merge_kernel(x_ref, ya_ref, yb_ref, gpre_ref, wg_ref, wa_ref, wb_ref, wo_ref, gpost_ref,
                  o_ref):
    x = x_ref[...]
    D = x.shape[-1]
    h = _rms(x, gpre_ref[...], NORM_EPS).astype(BF16)
    ga = _dot(h, wg_ref[:, :D])
    a = _dot(ya_ref[...], wa_ref[...])
    merged = jax.nn.sigmoid(ga) * a
    gb = _dot(h, wg_ref[:, D:])
    b = _dot(yb_ref[...], wb_ref[...])
    merged = merged + jax.nn.sigmoid(gb) * b
    o = _dot(merged.astype(BF16), wo_ref[...])
    o_ref[...] = x + _rms(o, gpost_ref[...], NORM_EPS)


def _const_spec(shape):
    return pl.BlockSpec(shape, lambda i: (0,) * len(shape), pipeline_mode=pl.Buffered(1))


def _merge(x2, ya2, yb2, gpre, w_g, wa, wb, wo, gpost, tm=512):
    T, D = x2.shape
    return pl.pallas_call(
        _merge_kernel,
        grid=(T // tm,),
        in_specs=[
            pl.BlockSpec((tm, D), lambda i: (i, 0)),
            pl.BlockSpec((tm, MOBA_WIDTH), lambda i: (i, 0)),
            pl.BlockSpec((tm, DIFF_WIDTH), lambda i: (i, 0)),
            _const_spec((1, D)),
            _const_spec((D, 2 * D)),
            _const_spec((MOBA_WIDTH, D)),
            _const_spec((DIFF_WIDTH, D)),
            _const_spec((D, D)),
            _const_spec((1, D)),
        ],
        out_specs=pl.BlockSpec((tm, D), lambda i: (i, 0)),
        out_shape=jax.ShapeDtypeStruct((T, D), F32),
        compiler_params=pltpu.CompilerParams(
            dimension_semantics=("arbitrary",), vmem_limit_bytes=VMEM_LIMIT),
        name="merge",
    )(x2, ya2, yb2, gpre.reshape(1, D), w_g, wa, wb, wo, gpost.reshape(1, D))


def _ffn_kernel(x_ref, gpre_ref, wg_ref, wu_ref, wd_ref, gpost_ref, o_ref, a_ref, *, chunk):
    x = x_ref[...]
    h = _rms(x, gpre_ref[...], NORM_EPS).astype(BF16)
    for c in range(D_FF // chunk):
        cols = slice(c * chunk, (c + 1) * chunk)
        g = _dot(h, wg_ref[:, cols])
        u = _dot(h, wu_ref[:, cols])
        a_ref[:, cols] = (jax.nn.silu(g) * u).astype(BF16)
    f = _dot(a_ref[...], wd_ref[...])
    o_ref[...] = x + _rms(f, gpost_ref[...], NORM_EPS)


def _ffn(x2, gpre, wg, wu, wd, gpost, tm=512):
    T, D = x2.shape
    return pl.pallas_call(
        functools.partial(_ffn_kernel, chunk=D_FF // 2),
        grid=(T // tm,),
        in_specs=[
            pl.BlockSpec((tm, D), lambda i: (i, 0)),
            _const_spec((1, D)),
            _const_spec((D, D_FF)),
            _const_spec((D, D_FF)),
            _const_spec((D_FF, D)),
            _const_spec((1, D)),
        ],
        out_specs=pl.BlockSpec((tm, D), lambda i: (i, 0)),
        out_shape=jax.ShapeDtypeStruct((T, D), F32),
        scratch_shapes=[pltpu.VMEM((tm, D_FF), BF16)],
        compiler_params=pltpu.CompilerParams(
            dimension_semantics=("arbitrary",), vmem_limit_bytes=VMEM_LIMIT),
        name="ffn",
    )(x2, gpre.reshape(1, D), wg, wu, wd, gpost.reshape(1, D))


def kernel(x, norm_mix_pre_g, w_in, w_branch_a, w_branch_b, lam_q1, lam_k1, lam_q2, lam_k2,
           diff_subln_g, w_out, norm_mix_post_g, norm_ffn_pre_g, w_gate, w_up, w_down,
           norm_ffn_post_g):
    B, S, D = x.shape
    assert D == D_MODEL and S % BLK == 0
    nb = S // BLK
    moba_slopes, diff_slopes = _alibi_slopes()
    qx_m, kx = _attn_tables(moba_slopes, S)
    qx_d, _ = _attn_tables(diff_slopes, S)
    qx_m = qx_m.reshape(MOBA_HEADS // 2, 2, 2, nb, LANES)[:, np.arange(2), np.arange(2)]
    qx_m, qx_d = jnp.asarray(qx_m), jnp.asarray(qx_d)
    kx = jnp.asarray(kx, BF16)
    depth = w_in.shape[0]
    for l in range(depth):
        lam_init = 0.8 - 0.6 * math.exp(-0.3 * l)
        w_qkv = w_in[l, :, :QKV_COLS].astype(BF16)
        w_g = w_in[l, :, QKV_COLS:].astype(BF16)
        qkv = _in_proj(x, norm_mix_pre_g[l], w_qkv)
        ya = _moba(qkv, qx_m, kx)
        yb = _diff(qkv, qx_d, kx, lam_q1[l], lam_k1[l], lam_q2[l], lam_k2[l],
                   diff_subln_g[l], lam_init)
        x2 = _merge(x.reshape(B * S, D), ya.reshape(B * S, MOBA_WIDTH),
                    yb.reshape(B * S, DIFF_WIDTH), norm_mix_pre_g[l], w_g,
                    w_branch_a[l].astype(BF16), w_branch_b[l].astype(BF16),
                    w_out[l].astype(BF16), norm_mix_post_g[l])
        x2 = _ffn(x2, norm_ffn_pre_g[l], w_gate[l].astype(BF16), w_up[l].astype(BF16),
                  w_down[l].astype(BF16), norm_ffn_post_g[l])
        x = x2.reshape(B, S, D)
    return x
```

```python
import functools
import math

import numpy as np
import jax
import jax.numpy as jnp
from jax import lax
from jax.experimental import pallas as pl
from jax.experimental.pallas import tpu as pltpu

D_MODEL = 1024
MOBA_HEADS = 8
MOBA_TOPK = 3
MOBA_WIDTH = 512
DIFF_HEADS = 4
DIFF_HEAD_DIM = 64
DIFF_WIDTH = 512
HEAD_DIM = 64
QKV_COLS = 3 * MOBA_WIDTH + 3 * DIFF_WIDTH
D_FF = 2816
NORM_EPS = 1e-6
SUBLN_EPS = 1e-5
ALIBI_MAX_BIAS = 8.0
NEG = -1e30
LOG2E = math.log2(math.e)
Q_SCALE = HEAD_DIM ** -0.5 * LOG2E

LANES = 128
BLK = 256
N_COL_TILES = QKV_COLS // LANES
ONES_ROWS = 16
EXTRA_PEN = 16
VMEM_LIMIT = 56 * 1024 * 1024

BF16 = jnp.bfloat16
F32 = jnp.float32


def _alibi_slopes():
    n = MOBA_HEADS + DIFF_HEADS
    slopes = 2.0 ** (-ALIBI_MAX_BIAS * np.arange(1, n + 1) / n)
    stride = n // DIFF_HEADS
    diff_idx = np.arange(DIFF_HEADS) * stride + (stride - 1)
    moba_idx = np.setdiff1d(np.arange(n), diff_idx)
    return (np.asarray(slopes[moba_idx], np.float32), np.asarray(slopes[diff_idx], np.float32))


def _split3(v):
    parts, rem = [], float(v)
    for _ in range(3):
        p = float(np.asarray(rem, np.float32).astype(BF16).astype(np.float64))
        parts.append(p)
        rem -= p
    return parts


def _attn_tables(slopes, S):
    nb = S // BLK
    H = len(slopes)
    qx = np.zeros((H, 2, nb, LANES), np.float32)
    kx = np.zeros((2, S, LANES), np.float32)
    pos = np.arange(S)
    for half in range(2):
        e0 = HEAD_DIM * (1 - half)
        for c in range(3):
            kx[half, :, e0 + c] = pos % BLK
            kx[half, :, e0 + 3 + c] = (pos // BLK) * BLK
            kx[half, :, e0 + 6 + c] = 1.0
        for n in range(nb):
            kx[half, n * BLK:(n + 1) * BLK, e0 + EXTRA_PEN + n] = 1.0
        for h in range(H):
            s = float(np.float32(slopes[h])) * LOG2E
            sp = _split3(s)
            for j in range(nb):
                cp = _split3(-s * j * BLK)
                for c in range(3):
                    qx[h, half, j, e0 + c] = sp[c]
                    qx[h, half, j, e0 + 3 + c] = sp[c]
                    qx[h, half, j, e0 + 6 + c] = cp[c]
    return qx, kx


def _dot(a, b):
    return jnp.dot(a, b, preferred_element_type=F32)


def _dot_nt(a, b):
    return lax.dot_general(a, b, (((1,), (1,)), ((), ())), preferred_element_type=F32)


def _rms(x, g, eps):
    return x * lax.rsqrt(jnp.mean(x * x, axis=-1, keepdims=True) + eps) * g


def _in_proj_kernel(x_ref, g_ref, w_ref, o_ref, *, chunk):
    h = _rms(x_ref[0], g_ref[...], NORM_EPS).astype(BF16)
    tiles_per_chunk = chunk // LANES
    for c in range(QKV_COLS // chunk):
        acc = _dot(h, w_ref[:, c * chunk:(c + 1) * chunk])
        for t in range(tiles_per_chunk):
            tile = c * tiles_per_chunk + t
            val = acc[:, t * LANES:(t + 1) * LANES]
            if tile < 4 or 12 <= tile < 16:
                val = val * Q_SCALE
            o_ref[0, tile] = val.astype(BF16)


def _in_proj(x, g, w_qkv, tm=512):
    B, S, D = x.shape
    return pl.pallas_call(
        functools.partial(_in_proj_kernel, chunk=512),
        grid=(B, S // tm),
        in_specs=[
            pl.BlockSpec((1, tm, D), lambda b, i: (b, i, 0)),
            pl.BlockSpec((1, D), lambda b, i: (0, 0)),
            pl.BlockSpec((D, QKV_COLS), lambda b, i: (0, 0)),
        ],
        out_specs=pl.BlockSpec((1, N_COL_TILES, tm, LANES), lambda b, i: (b, 0, i, 0)),
        out_shape=jax.ShapeDtypeStruct((B, N_COL_TILES, S, LANES), BF16),
        compiler_params=pltpu.CompilerParams(
            dimension_semantics=("arbitrary", "arbitrary"), vmem_limit_bytes=VMEM_LIMIT),
        name="in_proj",
    )(x, g.reshape(1, D), w_qkv)


def _causal_tile():
    rk = lax.broadcasted_iota(jnp.int32, (BLK, BLK), 0)
    rq = lax.broadcasted_iota(jnp.int32, (BLK, BLK), 1)
    return jnp.where(rk <= rq, 0.0, NEG)


def _scores_t(qa, ka_ref, idx, j, causal):
    lo, hi = j * BLK, (j + 1) * BLK
    sd = _dot_nt(ka_ref[idx, lo:hi, :], qa) + causal
    m = jnp.max(sd, axis=0, keepdims=True)
    sp = None
    if j > 0:
        sp = _dot_nt(ka_ref[idx, :lo, :], qa)
        m = jnp.maximum(m, jnp.max(sp, axis=0, keepdims=True))
    return sd, sp, m


def _weighted_values_t(scores, vt_ref, vidx, j):
    sd, sp, m = scores
    lo, hi = j * BLK, (j + 1) * BLK
    acc = _dot(vt_ref[vidx, :, lo:hi], jnp.exp2(sd - m).astype(BF16))
    if sp is not None:
        acc = acc + _dot(vt_ref[vidx, :, :lo], jnp.exp2(sp - m).astype(BF16))
    return acc


def _pipelined(items, first, second):
    pending = None
    for item in list(items) + [None]:
        nxt = (item, first(item)) if item is not None else None
        if pending is not None:
            second(*pending)
        pending = nxt


def _store_vt(v2, vt_ref, rows_of):
    S = v2.shape[0]
    vt = v2.astype(F32).T
    ones = jnp.ones((ONES_ROWS, S), F32)
    for idx, (lo, hi) in enumerate(rows_of):
        vt_ref[idx] = jnp.concatenate([vt[lo:hi], ones], axis=0).astype(BF16)


def _moba_kernel(q_ref, k_ref, v_ref, qx_ref, kx_ref, o_ref, ka_ref, vt_ref):
    S = q_ref.shape[2]
    nb = S // BLK
    lane = lax.broadcasted_iota(jnp.int32, (1, LANES), 1)
    row8 = lax.broadcasted_iota(jnp.int32, (nb, BLK), 0)
    causal = _causal_tile()

    kf = k_ref[0, 0].astype(F32)
    in_head, gates = [], []
    for hh in range(2):
        in_head.append((lane >= hh * HEAD_DIM) & (lane < (hh + 1) * HEAD_DIM))
        ka_ref[hh] = jnp.where(in_head[hh], kf, kx_ref[hh].astype(F32)).astype(BF16)
        khf = jnp.where(in_head[hh], kf, 0.0)
        kmean = jnp.sum(khf.reshape(nb, BLK, LANES), axis=1) * (1.0 / BLK)
        km_hi = kmean.astype(BF16)
        km_lo = (kmean - km_hi.astype(F32)).astype(BF16)
        g2 = _dot_nt(jnp.concatenate([km_hi, km_lo], axis=0), q_ref[0, 0])
        gates.append(g2[:nb] + g2[nb:])
    _store_vt(v_ref[0, 0], vt_ref, [(0, HEAD_DIM), (HEAD_DIM, 2 * HEAD_DIM)])

    n_sel = max(1, min(MOBA_TOPK, nb - 1))

    def scores(item):
        j, hh = item
        qjf = q_ref[0, 0, j * BLK:(j + 1) * BLK, :].astype(F32)
        ext = qx_ref[0, hh, j:j + 1, :]
        if j > 0:
            gj = gates[hh][:, j * BLK:(j + 1) * BLK]
            cnt = jnp.zeros((nb, BLK), F32)
            for m_ in range(j):
                gm = gj[m_:m_ + 1, :]
                beats = (gm > gj) | ((gm == gj) & (m_ < row8))
                cnt = cnt + beats.astype(F32)
            pen = jnp.where((row8 >= j) | (cnt < n_sel), 0.0, NEG)
            off = HEAD_DIM * (1 - hh) + EXTRA_PEN
            pen_rows = jnp.concatenate(
                [jnp.zeros((off, BLK), F32), pen, jnp.zeros((LANES - off - nb, BLK), F32)],
                axis=0)
            ext = ext + pen_rows.T
        qa = jnp.where(in_head[hh], qjf, ext).astype(BF16)
        return _scores_t(qa, ka_ref, hh, j, causal)

    halves = {}

    def finish(item, sc):
        j, hh = item
        acc = _weighted_values_t(sc, vt_ref, hh, j)
        halves[hh] = acc[:HEAD_DIM] / acc[HEAD_DIM:HEAD_DIM + 1]
        if hh == 1:
            o_t = jnp.concatenate([halves[0], halves[1]], axis=0)
            o_ref[0, j * BLK:(j + 1) * BLK, :] = o_t.T.astype(BF16)

    _pipelined([(j, hh) for j in range(nb) for hh in range(2)], scores, finish)


def _moba(qkv, qx, kx):
    B, _, S, _ = qkv.shape
    nb = S // BLK
    n_pairs = MOBA_HEADS // 2
    blk = lambda off: pl.BlockSpec((1, 1, S, LANES), lambda b, h: (b, off + h, 0, 0))
    return pl.pallas_call(
        _moba_kernel,
        grid=(B, n_pairs),
        in_specs=[
            blk(0), blk(n_pairs), blk(2 * n_pairs),
            pl.BlockSpec((1, 2, nb, LANES), lambda b, h: (h, 0, 0, 0)),
            pl.BlockSpec((2, S, LANES), lambda b, h: (0, 0, 0)),
        ],
        out_specs=pl.BlockSpec((1, S, LANES), lambda b, h: (b, 0, h)),
        out_shape=jax.ShapeDtypeStruct((B, S, MOBA_WIDTH), BF16),
        scratch_shapes=[
            pltpu.VMEM((2, S, LANES), BF16),
            pltpu.VMEM((2, HEAD_DIM + ONES_ROWS, S), BF16),
        ],
        compiler_params=pltpu.CompilerParams(
            dimension_semantics=("arbitrary", "arbitrary"), vmem_limit_bytes=VMEM_LIMIT),
        name="moba",
    )(qkv, qkv, qkv, qx, kx)


def _diff_kernel(q_ref, k_ref, v_ref, qx_ref, kx_ref, lq1_ref, lk1_ref, lq2_ref, lk2_ref, g_ref,
                 o_ref, ka_ref, vt_ref, *, lam_init):
    S = q_ref.shape[2]
    nb = S // BLK
    lane = lax.broadcasted_iota(jnp.int32, (1, LANES), 1)
    causal = _causal_tile()

    lam = (jnp.exp(jnp.sum(lq1_ref[...] * lk1_ref[...], axis=-1, keepdims=True))
           - jnp.exp(jnp.sum(lq2_ref[...] * lk2_ref[...], axis=-1, keepdims=True))
           + lam_init)

    kf = k_ref[0, 0].astype(F32)
    in_map = []
    for mp in range(2):
        in_map.append((lane >= mp * DIFF_HEAD_DIM) & (lane < (mp + 1) * DIFF_HEAD_DIM))
        ka_ref[mp] = jnp.where(in_map[mp], kf, kx_ref[mp].astype(F32)).astype(BF16)
    _store_vt(v_ref[0, 0], vt_ref, [(0, 2 * DIFF_HEAD_DIM)])

    dv = 2 * DIFF_HEAD_DIM

    def scores(item):
        j, mp = item
        qjf = q_ref[0, 0, j * BLK:(j + 1) * BLK, :].astype(F32)
        qa = jnp.where(in_map[mp], qjf, qx_ref[0, mp, j:j + 1, :]).astype(BF16)
        return _scores_t(qa, ka_ref, mp, j, causal)

    outs = {}

    def finish(item, sc):
        j, mp = item
        acc = _weighted_values_t(sc, vt_ref, 0, j)
        outs[mp] = acc[:dv] / acc[dv:dv + 1]
        if mp == 1:
            a = (outs[0] - lam * outs[1]).T
            y = _rms(a, g_ref[...], SUBLN_EPS) * (1.0 - lam_init)
            o_ref[0, j * BLK:(j + 1) * BLK, :] = y.astype(BF16)

    _pipelined([(j, mp) for j in range(nb) for mp in range(2)], scores, finish)


def _diff(qkv, qx, kx, lq1, lk1, lq2, lk2, subln_g, lam_init):
    B, _, S, _ = qkv.shape
    nb = S // BLK
    H = DIFF_HEADS
    base = 3 * (MOBA_HEADS // 2)
    blk = lambda off: pl.BlockSpec((1, 1, S, LANES), lambda b, h: (b, off + h, 0, 0))
    vec = lambda n: pl.BlockSpec((1, n), lambda b, h: (0, 0))
    d = DIFF_HEAD_DIM
    return pl.pallas_call(
        functools.partial(_diff_kernel, lam_init=lam_init),
        grid=(B, H),
        in_specs=[
            blk(base), blk(base + H), blk(base + 2 * H),
            pl.BlockSpec((1, 2, nb, LANES), lambda b, h: (h, 0, 0, 0)),
            pl.BlockSpec((2, S, LANES), lambda b, h: (0, 0, 0)),
            vec(d), vec(d), vec(d), vec(d), vec(2 * d),
        ],
        out_specs=pl.BlockSpec((1, S, LANES), lambda b, h: (b, 0, h)),
        out_shape=jax.ShapeDtypeStruct((B, S, DIFF_WIDTH), BF16),
        scratch_shapes=[
            pltpu.VMEM((2, S, LANES), BF16),
            pltpu.VMEM((1, 2 * d + ONES_ROWS, S), BF16),
        ],
        compiler_params=pltpu.CompilerParams(
            dimension_semantics=("arbitrary", "arbitrary"), vmem_limit_bytes=VMEM_LIMIT),
        name="diff_attn",
    )(qkv, qkv, qkv, qx, kx, lq1.reshape(1, d), lk1.reshape(1, d), lq2.reshape(1, d),
      lk2.reshape(1, d), subln_g.reshape(1, 2 * d))


def _merge_kernel(x_ref, ya_ref, yb_ref, gpre_ref, wg_ref, wa_ref, wb_ref, wo_ref, gpost_ref,
                  o_ref):
    x = x_ref[...]
    D = x.shape[-1]
    h = _rms(x, gpre_ref[...], NORM_EPS).astype(BF16)
    ga = _dot(h, wg_ref[:, :D])
    a = _dot(ya_ref[...], wa_ref[...])
    merged = jax.nn.sigmoid(ga) * a
    gb = _dot(h, wg_ref[:, D:])
    b = _dot(yb_ref[...], wb_ref[...])
    merged = merged + jax.nn.sigmoid(gb) * b
    o = _dot(merged.astype(BF16), wo_ref[...])
    o_ref[...] = x + _rms(o, gpost_ref[...], NORM_EPS)


def _const_spec(shape):
    return pl.BlockSpec(shape, lambda i: (0,) * len(shape), pipeline_mode=pl.Buffered(1))


def _merge(x2, ya2, yb2, gpre, w_g, wa, wb, wo, gpost, tm=512):
    T, D = x2.shape
    return pl.pallas_call(
        _merge_kernel,
        grid=(T // tm,),
        in_specs=[
            pl.BlockSpec((tm, D), lambda i: (i, 0)),
            pl.BlockSpec((tm, MOBA_WIDTH), lambda i: (i, 0)),
            pl.BlockSpec((tm, DIFF_WIDTH), lambda i: (i, 0)),
            _const_spec((1, D)),
            _const_spec((D, 2 * D)),
            _const_spec((MOBA_WIDTH, D)),
            _const_spec((DIFF_WIDTH, D)),
            _const_spec((D, D)),
            _const_spec((1, D)),
        ],
        out_specs=pl.BlockSpec((tm, D), lambda i: (i, 0)),
        out_shape=jax.ShapeDtypeStruct((T, D), F32),
        compiler_params=pltpu.CompilerParams(
            dimension_semantics=("arbitrary",), vmem_limit_bytes=VMEM_LIMIT),
        name="merge",
    )(x2, ya2, yb2, gpre.reshape(1, D), w_g, wa, wb, wo, gpost.reshape(1, D))


def _ffn_kernel(x_ref, gpre_ref, wg_ref, wu_ref, wd_ref, gpost_ref, o_ref, a_ref, *, chunk):
    x = x_ref[...]
    h = _rms(x, gpre_ref[...], NORM_EPS).astype(BF16)
    for c in range(D_FF // chunk):
        cols = slice(c * chunk, (c + 1) * chunk)
        g = _dot(h, wg_ref[:, cols])
        u = _dot(h, wu_ref[:, cols])
        a_ref[:, cols] = (jax.nn.silu(g) * u).astype(BF16)
    f = _dot(a_ref[...], wd_ref[...])
    o_ref[...] = x + _rms(f, gpost_ref[...], NORM_EPS)


def _ffn(x2, gpre, wg, wu, wd, gpost, tm=512):
    T, D = x2.shape
    return pl.pallas_call(
        functools.partial(_ffn_kernel, chunk=D_FF // 2),
        grid=(T // tm,),
        in_specs=[
            pl.BlockSpec((tm, D), lambda i: (i, 0)),
            _const_spec((1, D)),
            _const_spec((D, D_FF)),
            _const_spec((D, D_FF)),
            _const_spec((D_FF, D)),
            _const_spec((1, D)),
        ],
        out_specs=pl.BlockSpec((tm, D), lambda i: (i, 0)),
        out_shape=jax.ShapeDtypeStruct((T, D), F32),
        scratch_shapes=[pltpu.VMEM((tm, D_FF), BF16)],
        compiler_params=pltpu.CompilerParams(
            dimension_semantics=("arbitrary",), vmem_limit_bytes=VMEM_LIMIT),
        name="ffn",
    )(x2, gpre.reshape(1, D), wg, wu, wd, gpost.reshape(1, D))


def kernel(x, norm_mix_pre_g, w_in, w_branch_a, w_branch_b, lam_q1, lam_k1, lam_q2, lam_k2,
           diff_subln_g, w_out, norm_mix_post_g, norm_ffn_pre_g, w_gate, w_up, w_down,
           norm_ffn_post_g):
    B, S, D = x.shape
    assert D == D_MODEL and S % BLK == 0
    nb = S // BLK
    moba_slopes, diff_slopes = _alibi_slopes()
    qx_m, kx = _attn_tables(moba_slopes, S)
    qx_d, _ = _attn_tables(diff_slopes, S)
    qx_m = qx_m.reshape(MOBA_HEADS // 2, 2, 2, nb, LANES)[:, np.arange(2), np.arange(2)]
    qx_m, qx_d = jnp.asarray(qx_m), jnp.asarray(qx_d)
    kx = jnp.asarray(kx, BF16)
    depth = w_in.shape[0]
    for l in range(depth):
        lam_init = 0.8 - 0.6 * math.exp(-0.3 * l)
        w_qkv = w_in[l, :, :QKV_COLS].astype(BF16)
        w_g = w_in[l, :, QKV_COLS:].astype(BF16)
        qkv = _in_proj(x, norm_mix_pre_g[l], w_qkv)
        ya = _moba(qkv, qx_m, kx)
        yb = _diff(qkv, qx_d, kx, lam_q1[l], lam_k1[l], lam_q2[l], lam_k2[l],
                   diff_subln_g[l], lam_init)
        x2 = _merge(x.reshape(B * S, D), ya.reshape(B * S, MOBA_WIDTH),
                    yb.reshape(B * S, DIFF_WIDTH), norm_mix_pre_g[l], w_g,
                    w_branch_a[l].astype(BF16), w_branch_b[l].astype(BF16),
                    w_out[l].astype(BF16), norm_mix_post_g[l])
        x2 = _ffn(x2, norm_ffn_pre_g[l], w_gate[l].astype(BF16), w_up[l].astype(BF16),
                  w_down[l].astype(BF16), norm_ffn_post_g[l])
        x = x2.reshape(B, S, D)
    return x
```

```python
import functools
import math

import numpy as np
import jax
import jax.numpy as jnp
from jax import lax
from jax.experimental import pallas as pl
from jax.experimental.pallas import tpu as pltpu

D_MODEL = 1024
MOBA_HEADS = 8
MOBA_TOPK = 3
MOBA_WIDTH = 512
DIFF_HEADS = 4
DIFF_HEAD_DIM = 64
DIFF_WIDTH = 512
HEAD_DIM = 64
QKV_COLS = 3 * MOBA_WIDTH + 3 * DIFF_WIDTH
D_FF = 2816
NORM_EPS = 1e-6
SUBLN_EPS = 1e-5
ALIBI_MAX_BIAS = 8.0
NEG = -1e30
LOG2E = math.log2(math.e)
Q_SCALE = HEAD_DIM ** -0.5 * LOG2E

LANES = 128
BLK = 256
N_COL_TILES = QKV_COLS // LANES
ONES_ROWS = 16
EXTRA_PEN = 16
GROUP = 2
SLOTS = 2 * GROUP
VMEM_LIMIT = 56 * 1024 * 1024

BF16 = jnp.bfloat16
F32 = jnp.float32


def _alibi_slopes():
    n = MOBA_HEADS + DIFF_HEADS
    slopes = 2.0 ** (-ALIBI_MAX_BIAS * np.arange(1, n + 1) / n)
    stride = n // DIFF_HEADS
    diff_idx = np.arange(DIFF_HEADS) * stride + (stride - 1)
    moba_idx = np.setdiff1d(np.arange(n), diff_idx)
    return (np.asarray(slopes[moba_idx], np.float32), np.asarray(slopes[diff_idx], np.float32))


def _split3(v):
    parts, rem = [], float(v)
    for _ in range(3):
        p = float(np.asarray(rem, np.float32).astype(BF16).astype(np.float64))
        parts.append(p)
        rem -= p
    return parts


def _attn_tables(slopes, S):
    nb = S // BLK
    H = len(slopes)
    qx = np.zeros((H, 2, nb, LANES), np.float32)
    kx = np.zeros((2, S, LANES), np.float32)
    pos = np.arange(S)
    for half in range(2):
        e0 = HEAD_DIM * (1 - half)
        for c in range(3):
            kx[half, :, e0 + c] = pos % BLK
            kx[half, :, e0 + 3 + c] = (pos // BLK) * BLK
            kx[half, :, e0 + 6 + c] = 1.0
        for n in range(nb):
            kx[half, n * BLK:(n + 1) * BLK, e0 + EXTRA_PEN + n] = 1.0
        for h in range(H):
            s = float(np.float32(slopes[h])) * LOG2E
            sp = _split3(s)
            for j in range(nb):
                cp = _split3(-s * j * BLK)
                for c in range(3):
                    qx[h, half, j, e0 + c] = sp[c]
                    qx[h, half, j, e0 + 3 + c] = sp[c]
                    qx[h, half, j, e0 + 6 + c] = cp[c]
    return qx.transpose(0, 1, 3, 2), kx


def _dot(a, b):
    return jnp.dot(a, b, preferred_element_type=F32)


def _rms(x, g, eps):
    return x * lax.rsqrt(jnp.mean(x * x, axis=-1, keepdims=True) + eps) * g


def _in_proj_kernel(x_ref, g_ref, w_ref, o_ref, *, chunk):
    h = _rms(x_ref[0], g_ref[...], NORM_EPS).astype(BF16)
    tiles_per_chunk = chunk // LANES
    for c in range(QKV_COLS // chunk):
        acc = _dot(h, w_ref[:, c * chunk:(c + 1) * chunk])
        for t in range(tiles_per_chunk):
            tile = c * tiles_per_chunk + t
            val = acc[:, t * LANES:(t + 1) * LANES]
            if tile < 4 or 12 <= tile < 16:
                val = val * Q_SCALE
            o_ref[0, tile] = val.astype(BF16)


def _in_proj(x, g, w_qkv, tm=512):
    B, S, D = x.shape
    return pl.pallas_call(
        functools.partial(_in_proj_kernel, chunk=512),
        grid=(B, S // tm),
        in_specs=[
            pl.BlockSpec((1, tm, D), lambda b, i: (b, i, 0)),
            pl.BlockSpec((1, D), lambda b, i: (0, 0)),
            pl.BlockSpec((D, QKV_COLS), lambda b, i: (0, 0)),
        ],
        out_specs=pl.BlockSpec((1, N_COL_TILES, tm, LANES), lambda b, i: (b, 0, i, 0)),
        out_shape=jax.ShapeDtypeStruct((B, N_COL_TILES, S, LANES), BF16),
        compiler_params=pltpu.CompilerParams(
            dimension_semantics=("arbitrary", "arbitrary"), vmem_limit_bytes=VMEM_LIMIT),
        name="in_proj",
    )(x, g.reshape(1, D), w_qkv)


def _store_causal(causal_ref):
    rk = lax.broadcasted_iota(jnp.int32, (BLK, BLK), 0)
    rq = lax.broadcasted_iota(jnp.int32, (BLK, BLK), 1)
    causal_ref[...] = jnp.where(rk <= rq, 0.0, NEG)


def _scores_t(qa_ref, ka_ref, causal_ref, s_ref, slot, idx, j, out):
    m8 = None
    for n in range(j, -1, -1):
        s = _dot(ka_ref[idx, n * BLK:(n + 1) * BLK, :], qa_ref[slot])
        if n == j:
            s = s + causal_ref[...]
        s_ref[slot, n] = s
        t8 = jnp.max(s.reshape(BLK // 8, 8, BLK), axis=0)
        m8 = t8 if m8 is None else jnp.maximum(m8, t8)
        yield
    out['m'] = jnp.max(m8, axis=0, keepdims=True)


def _weighted_values_t(m, s_ref, slot, vt_ref, vidx, j, out):
    acc = None
    for n in range(j, -1, -1):
        p = jnp.exp2(s_ref[slot, n] - m).astype(BF16)
        d = _dot(vt_ref[vidx, :, n * BLK:(n + 1) * BLK], p)
        acc = d if acc is None else acc + d
        yield
    out['acc'] = acc


def _pipelined(items, first, second, last):
    pending = []
    groups = [items[i:i + GROUP] for i in range(0, len(items), GROUP)]
    for gi, group in enumerate(groups + [[]]):
        gens, started, results = [], [], []
        for t, item in enumerate(group):
            sc = {}
            k = gi * GROUP + t
            gens.append(first(k, item, sc))
            started.append((k, item, sc))
        for k, item, sc in pending:
            res = {}
            gens.append(second(k, item, sc, res))
            results.append((item, res))
        while gens:
            for g in list(gens):
                if next(g, StopIteration) is StopIteration:
                    gens.remove(g)
        for item, res in results:
            last(item, res)
        pending = started


def _store_vt(v2, vt_ref, rows_of):
    S = v2.shape[0]
    vt = v2.astype(F32).T
    ones = jnp.ones((ONES_ROWS, S), F32)
    for idx, (lo, hi) in enumerate(rows_of):
        vt_ref[idx] = jnp.concatenate([vt[lo:hi], ones], axis=0).astype(BF16)


def _attn_scratch(S, vt_groups, vt_rows):
    nb = S // BLK
    return [
        pltpu.VMEM((2, S, LANES), BF16),
        pltpu.VMEM((vt_groups, vt_rows, S), BF16),
        pltpu.VMEM((LANES, S), F32),
        pltpu.VMEM((SLOTS, LANES, BLK), BF16),
        pltpu.VMEM((SLOTS, nb, BLK, BLK), F32),
        pltpu.VMEM((BLK, BLK), F32),
    ]


def _moba_kernel(q_ref, k_ref, v_ref, qx_ref, kx_ref, o_ref,
                 ka_ref, vt_ref, qt_ref, qa_ref, s_ref, causal_ref, g_ref, h_ref):
    S = q_ref.shape[2]
    nb = S // BLK
    lane = lax.broadcasted_iota(jnp.int32, (1, LANES), 1)
    row = lax.broadcasted_iota(jnp.int32, (LANES, 1), 0)
    row8 = lax.broadcasted_iota(jnp.int32, (nb, BLK), 0)
    _store_causal(causal_ref)

    kf = k_ref[0, 0].astype(F32)
    qt_ref[...] = q_ref[0, 0].astype(F32).T
    for hh in range(2):
        in_head = (lane >= hh * HEAD_DIM) & (lane < (hh + 1) * HEAD_DIM)
        ka_ref[hh] = jnp.where(in_head, kf, kx_ref[hh].astype(F32)).astype(BF16)
        khf = jnp.where(in_head, kf, 0.0)
        kmean = jnp.sum(khf.reshape(nb, BLK, LANES), axis=1) * (1.0 / BLK)
        km_hi = kmean.astype(BF16)
        km_lo = (kmean - km_hi.astype(F32)).astype(BF16)
        g2 = _dot(jnp.concatenate([km_hi, km_lo], axis=0), qt_ref[...].astype(BF16))
        g_ref[hh] = g2[:nb] + g2[nb:]
    _store_vt(v_ref[0, 0], vt_ref, [(0, HEAD_DIM), (HEAD_DIM, 2 * HEAD_DIM)])

    n_sel = max(1, min(MOBA_TOPK, nb - 1))

    def scores(k, item, sc):
        j, hh = item
        cols = slice(j * BLK, (j + 1) * BLK)
        ext = qx_ref[0, hh][:, j:j + 1]
        if j > 0:
            gj = g_ref[hh, :, cols]
            cnt = jnp.zeros((nb, BLK), F32)
            for m_ in range(j):
                gm = gj[m_:m_ + 1, :]
                beats = (gm > gj) | ((gm == gj) & (m_ < row8))
                cnt = cnt + beats.astype(F32)
            pen = jnp.where((row8 >= j) | (cnt < n_sel), 0.0, NEG)
            off = HEAD_DIM * (1 - hh) + EXTRA_PEN
            ext = ext + jnp.concatenate(
                [jnp.zeros((off, BLK), F32), pen, jnp.zeros((LANES - off - nb, BLK), F32)],
                axis=0)
        head_rows = (row >= hh * HEAD_DIM) & (row < (hh + 1) * HEAD_DIM)
        qa_ref[k % SLOTS] = jnp.where(head_rows, qt_ref[:, cols], ext).astype(BF16)
        yield from _scores_t(qa_ref, ka_ref, causal_ref, s_ref, k % SLOTS, hh, j, sc)

    def values(k, item, sc, res):
        j, hh = item
        yield from _weighted_values_t(sc['m'], s_ref, k % SLOTS, vt_ref, hh, j, res)

    def finish(item, res):
        j, hh = item
        acc = res['acc']
        h_ref[hh * HEAD_DIM:(hh + 1) * HEAD_DIM, :] = acc[:HEAD_DIM] / acc[HEAD_DIM:HEAD_DIM + 1]
        if hh == 1:
            o_ref[0, j * BLK:(j + 1) * BLK, :] = h_ref[...].T.astype(BF16)

    _pipelined([(j, hh) for j in reversed(range(nb)) for hh in range(2)], scores, values, finish)


def _moba(qkv, qx, kx):
    B, _, S, _ = qkv.shape
    nb = S // BLK
    n_pairs = MOBA_HEADS // 2
    blk = lambda off: pl.BlockSpec((1, 1, S, LANES), lambda b, h: (b, off + h, 0, 0))
    return pl.pallas_call(
        _moba_kernel,
        grid=(B, n_pairs),
        in_specs=[
            blk(0), blk(n_pairs), blk(2 * n_pairs),
            pl.BlockSpec((1, 2, LANES, nb), lambda b, h: (h, 0, 0, 0)),
            pl.BlockSpec((2, S, LANES), lambda b, h: (0, 0, 0)),
        ],
        out_specs=pl.BlockSpec((1, S, LANES), lambda b, h: (b, 0, h)),
        out_shape=jax.ShapeDtypeStruct((B, S, MOBA_WIDTH), BF16),
        scratch_shapes=_attn_scratch(S, 2, HEAD_DIM + ONES_ROWS) + [
            pltpu.VMEM((2, nb, S), F32),
            pltpu.VMEM((LANES, BLK), F32),
        ],
        compiler_params=pltpu.CompilerParams(
            dimension_semantics=("arbitrary", "arbitrary"), vmem_limit_bytes=VMEM_LIMIT),
        name="moba",
    )(qkv, qkv, qkv, qx, kx)


def _diff_kernel(q_ref, k_ref, v_ref, qx_ref, kx_ref, lq1_ref, lk1_ref, lq2_ref, lk2_ref, g_ref,
                 o_ref, ka_ref, vt_ref, qt_ref, qa_ref, s_ref, causal_ref, h_ref, *, lam_init):
    S = q_ref.shape[2]
    nb = S // BLK
    lane = lax.broadcasted_iota(jnp.int32, (1, LANES), 1)
    row = lax.broadcasted_iota(jnp.int32, (LANES, 1), 0)
    _store_causal(causal_ref)

    lam = (jnp.exp(jnp.sum(lq1_ref[...] * lk1_ref[...], axis=-1, keepdims=True))
           - jnp.exp(jnp.sum(lq2_ref[...] * lk2_ref[...], axis=-1, keepdims=True))
           + lam_init)

    kf = k_ref[0, 0].astype(F32)
    qt_ref[...] = q_ref[0, 0].astype(F32).T
    for mp in range(2):
        in_map = (lane >= mp * DIFF_HEAD_DIM) & (lane < (mp + 1) * DIFF_HEAD_DIM)
        ka_ref[mp] = jnp.where(in_map, kf, kx_ref[mp].astype(F32)).astype(BF16)
    _store_vt(v_ref[0, 0], vt_ref, [(0, 2 * DIFF_HEAD_DIM)])

    dv = 2 * DIFF_HEAD_DIM

    def scores(k, item, sc):
        j, mp = item
        map_rows = (row >= mp * DIFF_HEAD_DIM) & (row < (mp + 1) * DIFF_HEAD_DIM)
        ext = qx_ref[0, mp][:, j:j + 1]
        qa_ref[k % SLOTS] = jnp.where(map_rows, qt_ref[:, j * BLK:(j + 1) * BLK], ext).astype(BF16)
        yield from _scores_t(qa_ref, ka_ref, causal_ref, s_ref, k % SLOTS, mp, j, sc)

    def values(k, item, sc, res):
        yield from _weighted_values_t(sc['m'], s_ref, k % SLOTS, vt_ref, 0, item[0], res)

    def finish(item, res):
        j, mp = item
        acc = res['acc']
        o = acc[:dv] / acc[dv:dv + 1]
        if mp == 0:
            h_ref[...] = o
        else:
            a = (h_ref[...] - lam * o).T
            y = _rms(a, g_ref[...], SUBLN_EPS) * (1.0 - lam_init)
            o_ref[0, j * BLK:(j + 1) * BLK, :] = y.astype(BF16)

    _pipelined([(j, mp) for j in reversed(range(nb)) for mp in range(2)], scores, values, finish)


def _diff(qkv, qx, kx, lq1, lk1, lq2, lk2, subln_g, lam_init):
    B, _, S, _ = qkv.shape
    nb = S // BLK
    H = DIFF_HEADS
    base = 3 * (MOBA_HEADS // 2)
    blk = lambda off: pl.BlockSpec((1, 1, S, LANES), lambda b, h: (b, off + h, 0, 0))
    vec = lambda n: pl.BlockSpec((1, n), lambda b, h: (0, 0))
    d = DIFF_HEAD_DIM
    return pl.pallas_call(
        functools.partial(_diff_kernel, lam_init=lam_init),
        grid=(B, H),
        in_specs=[
            blk(base), blk(base + H), blk(base + 2 * H),
            pl.BlockSpec((1, 2, LANES, nb), lambda b, h: (h, 0, 0, 0)),
            pl.BlockSpec((2, S, LANES), lambda b, h: (0, 0, 0)),
            vec(d), vec(d), vec(d), vec(d), vec(2 * d),
        ],
        out_specs=pl.BlockSpec((1, S, LANES), lambda b, h: (b, 0, h)),
        out_shape=jax.ShapeDtypeStruct((B, S, DIFF_WIDTH), BF16),
        scratch_shapes=_attn_scratch(S, 1, 2 * d + ONES_ROWS) + [
            pltpu.VMEM((2 * d, BLK), F32),
        ],
        compiler_params=pltpu.CompilerParams(
            dimension_semantics=("arbitrary", "arbitrary"), vmem_limit_bytes=VMEM_LIMIT),
        name="diff_attn",
    )(qkv, qkv, qkv, qx, kx, lq1.reshape(1, d), lk1.reshape(1, d), lq2.reshape(1, d),
      lk2.reshape(1, d), subln_g.reshape(1, 2 * d))


def _merge_kernel(x_ref, ya_ref, yb_ref, gpre_ref, wg_ref, wa_ref, wb_ref, wo_ref, gpost_ref,
                  o_ref):
    x = x_ref[...]
    D = x.shape[-1]
    h = _rms(x, gpre_ref[...], NORM_EPS).astype(BF16)
    ga = _dot(h, wg_ref[:, :D])
    a = _dot(ya_ref[...], wa_ref[...])
    merged = jax.nn.sigmoid(ga) * a
    gb = _dot(h, wg_ref[:, D:])
    b = _dot(yb_ref[...], wb_ref[...])
    merged = merged + jax.nn.sigmoid(gb) * b
    o = _dot(merged.astype(BF16), wo_ref[...])
    o_ref[...] = x + _rms(o, gpost_ref[...], NORM_EPS)


def _const_spec(shape):
    return pl.BlockSpec(shape, lambda i: (0,) * len(shape), pipeline_mode=pl.Buffered(1))


def _merge(x2, ya2, yb2, gpre, w_g, wa, wb, wo, gpost, tm=512):
    T, D = x2.shape
    return pl.pallas_call(
        _merge_kernel,
        grid=(T // tm,),
        in_specs=[
            pl.BlockSpec((tm, D), lambda i: (i, 0)),
            pl.BlockSpec((tm, MOBA_WIDTH), lambda i: (i, 0)),
            pl.BlockSpec((tm, DIFF_WIDTH), lambda i: (i, 0)),
            _const_spec((1, D)),
            _const_spec((D, 2 * D)),
            _const_spec((MOBA_WIDTH, D)),
            _const_spec((DIFF_WIDTH, D)),
            _const_spec((D, D)),
            _const_spec((1, D)),
        ],
        out_specs=pl.BlockSpec((tm, D), lambda i: (i, 0)),
        out_shape=jax.ShapeDtypeStruct((T, D), F32),
        compiler_params=pltpu.CompilerParams(
            dimension_semantics=("arbitrary",), vmem_limit_bytes=VMEM_LIMIT),
        name="merge",
    )(x2, ya2, yb2, gpre.reshape(1, D), w_g, wa, wb, wo, gpost.reshape(1, D))


def _ffn_kernel(x_ref, gpre_ref, wg_ref, wu_ref, wd_ref, gpost_ref, o_ref, a_ref, *, chunk):
    x = x_ref[...]
    h = _rms(x, gpre_ref[...], NORM_EPS).astype(BF16)
    for c in range(D_FF // chunk):
        cols = slice(c * chunk, (c + 1) * chunk)
        g = _dot(h, wg_ref[:, cols])
        u = _dot(h, wu_ref[:, cols])
        a_ref[:, cols] = (jax.nn.silu(g) * u).astype(BF16)
    f = _dot(a_ref[...], wd_ref[...])
    o_ref[...] = x + _rms(f, gpost_ref[...], NORM_EPS)


def _ffn(x2, gpre, wg, wu, wd, gpost, tm=512):
    T, D = x2.shape
    return pl.pallas_call(
        functools.partial(_ffn_kernel, chunk=D_FF // 2),
        grid=(T // tm,),
        in_specs=[
            pl.BlockSpec((tm, D), lambda i: (i, 0)),
            _const_spec((1, D)),
            _const_spec((D, D_FF)),
            _const_spec((D, D_FF)),
            _const_spec((D_FF, D)),
            _const_spec((1, D)),
        ],
        out_specs=pl.BlockSpec((tm, D), lambda i: (i, 0)),
        out_shape=jax.ShapeDtypeStruct((T, D), F32),
        scratch_shapes=[pltpu.VMEM((tm, D_FF), BF16)],
        compiler_params=pltpu.CompilerParams(
            dimension_semantics=("arbitrary",), vmem_limit_bytes=VMEM_LIMIT),
        name="ffn",
    )(x2, gpre.reshape(1, D), wg, wu, wd, gpost.reshape(1, D))


def kernel(x, norm_mix_pre_g, w_in, w_branch_a, w_branch_b, lam_q1, lam_k1, lam_q2, lam_k2,
           diff_subln_g, w_out, norm_mix_post_g, norm_ffn_pre_g, w_gate, w_up, w_down,
           norm_ffn_post_g):
    B, S, D = x.shape
    assert D == D_MODEL and S % BLK == 0
    nb = S // BLK
    moba_slopes, diff_slopes = _alibi_slopes()
    qx_m, kx = _attn_tables(moba_slopes, S)
    qx_d, _ = _attn_tables(diff_slopes, S)
    qx_m = qx_m.reshape(MOBA_HEADS // 2, 2, 2, LANES, nb)[:, np.arange(2), np.arange(2)]
    qx_m, qx_d = jnp.asarray(qx_m), jnp.asarray(qx_d)
    kx = jnp.asarray(kx, BF16)
    depth = w_in.shape[0]
    for l in range(depth):
        lam_init = 0.8 - 0.6 * math.exp(-0.3 * l)
        w_qkv = w_in[l, :, :QKV_COLS].astype(BF16)
        w_g = w_in[l, :, QKV_COLS:].astype(BF16)
        qkv = _in_proj(x, norm_mix_pre_g[l], w_qkv)
        ya = _moba(qkv, qx_m, kx)
        yb = _diff(qkv, qx_d, kx, lam_q1[l], lam_k1[l], lam_q2[l], lam_k2[l],
                   diff_subln_g[l], lam_init)
        x2 = _merge(x.reshape(B * S, D), ya.reshape(B * S, MOBA_WIDTH),
                    yb.reshape(B * S, DIFF_WIDTH), norm_mix_pre_g[l], w_g,
                    w_branch_a[l].astype(BF16), w_branch_b[l].astype(BF16),
                    w_out[l].astype(BF16), norm_mix_post_g[l])
        x2 = _ffn(x2, norm_ffn_pre_g[l], w_gate[l].astype(BF16), w_up[l].astype(BF16),
                  w_down[l].astype(BF16), norm_ffn_post_g[l])
        x = x2.reshape(B, S, D)
    return x
```

```python
import functools
import math

import numpy as np
import jax
import jax.numpy as jnp
from jax import lax
from jax.experimental import pallas as pl
from jax.experimental.pallas import tpu as pltpu

D_MODEL = 1024
MOBA_HEADS = 8
MOBA_TOPK = 3
MOBA_WIDTH = 512
DIFF_HEADS = 4
DIFF_HEAD_DIM = 64
DIFF_WIDTH = 512
HEAD_DIM = 64
QKV_COLS = 3 * MOBA_WIDTH + 3 * DIFF_WIDTH
D_FF = 2816
NORM_EPS = 1e-6
SUBLN_EPS = 1e-5
ALIBI_MAX_BIAS = 8.0
NEG = -1e30
LOG2E = math.log2(math.e)
Q_SCALE = HEAD_DIM ** -0.5 * LOG2E

LANES = 128
BLK = 256
N_COL_TILES = QKV_COLS // LANES
ONES_ROWS = 16
EXTRA_PEN = 16
VMEM_LIMIT = 56 * 1024 * 1024

BF16 = jnp.bfloat16
F32 = jnp.float32


def _alibi_slopes():
    n = MOBA_HEADS + DIFF_HEADS
    slopes = 2.0 ** (-ALIBI_MAX_BIAS * np.arange(1, n + 1) / n)
    stride = n // DIFF_HEADS
    diff_idx = np.arange(DIFF_HEADS) * stride + (stride - 1)
    moba_idx = np.setdiff1d(np.arange(n), diff_idx)
    return (np.asarray(slopes[moba_idx], np.float32), np.asarray(slopes[diff_idx], np.float32))


def _split3(v):
    parts, rem = [], float(v)
    for _ in range(3):
        p = float(np.asarray(rem, np.float32).astype(BF16).astype(np.float64))
        parts.append(p)
        rem -= p
    return parts


def _attn_tables(slopes, S):
    nb = S // BLK
    H = len(slopes)
    qx = np.zeros((H, 2, nb, LANES), np.float32)
    kx = np.zeros((2, S, LANES), np.float32)
    pos = np.arange(S)
    for half in range(2):
        e0 = HEAD_DIM * (1 - half)
        for c in range(3):
            kx[half, :, e0 + c] = pos % BLK
            kx[half, :, e0 + 3 + c] = (pos // BLK) * BLK
            kx[half, :, e0 + 6 + c] = 1.0
        for n in range(nb):
            kx[half, n * BLK:(n + 1) * BLK, e0 + EXTRA_PEN + n] = 1.0
        for h in range(H):
            s = float(np.float32(slopes[h])) * LOG2E
            sp = _split3(s)
            for j in range(nb):
                cp = _split3(-s * j * BLK)
                for c in range(3):
                    qx[h, half, j, e0 + c] = sp[c]
                    qx[h, half, j, e0 + 3 + c] = sp[c]
                    qx[h, half, j, e0 + 6 + c] = cp[c]
    return qx.transpose(0, 1, 3, 2), kx


def _dot(a, b):
    return jnp.dot(a, b, preferred_element_type=F32)


def _rms(x, g, eps):
    return x * lax.rsqrt(jnp.mean(x * x, axis=-1, keepdims=True) + eps) * g


def _in_proj_kernel(x_ref, g_ref, w_ref, o_ref, *, chunk):
    h = _rms(x_ref[0], g_ref[...], NORM_EPS).astype(BF16)
    tiles_per_chunk = chunk // LANES
    for c in range(QKV_COLS // chunk):
        acc = _dot(h, w_ref[:, c * chunk:(c + 1) * chunk])
        for t in range(tiles_per_chunk):
            tile = c * tiles_per_chunk + t
            val = acc[:, t * LANES:(t + 1) * LANES]
            if tile < 4 or 12 <= tile < 16:
                val = val * Q_SCALE
            o_ref[0, tile] = val.astype(BF16)


def _in_proj(x, g, w_qkv, tm=512):
    B, S, D = x.shape
    return pl.pallas_call(
        functools.partial(_in_proj_kernel, chunk=512),
        grid=(B, S // tm),
        in_specs=[
            pl.BlockSpec((1, tm, D), lambda b, i: (b, i, 0)),
            pl.BlockSpec((1, D), lambda b, i: (0, 0)),
            pl.BlockSpec((D, QKV_COLS), lambda b, i: (0, 0)),
        ],
        out_specs=pl.BlockSpec((1, N_COL_TILES, tm, LANES), lambda b, i: (b, 0, i, 0)),
        out_shape=jax.ShapeDtypeStruct((B, N_COL_TILES, S, LANES), BF16),
        compiler_params=pltpu.CompilerParams(
            dimension_semantics=("arbitrary", "arbitrary"), vmem_limit_bytes=VMEM_LIMIT),
        name="in_proj",
    )(x, g.reshape(1, D), w_qkv)


def _store_causal(causal_ref):
    rk = lax.broadcasted_iota(jnp.int32, (BLK, BLK), 0)
    rq = lax.broadcasted_iota(jnp.int32, (BLK, BLK), 1)
    causal_ref[...] = jnp.where(rk <= rq, 0.0, NEG)


def _scores_t(qa_ref, ka_ref, causal_ref, s_ref, m_ref, item_id, par, slot0, idx, j):
    m8 = None
    for t, n in enumerate(range(j, -1, -1)):
        s = _dot(ka_ref[idx, n * BLK:(n + 1) * BLK, :], qa_ref[item_id])
        if n == j:
            s = s + causal_ref[...]
        s_ref[par, slot0 + t] = s
        t8 = jnp.max(s.reshape(BLK // 8, 8, BLK), axis=0)
        m8 = t8 if m8 is None else jnp.maximum(m8, t8)
        yield
    m_ref[item_id] = jnp.broadcast_to(jnp.max(m8, axis=0, keepdims=True), (8, BLK))


def _weighted_values_t(s_ref, m_ref, acc_ref, item_id, par, slot0, vt_ref, vidx, j):
    acc = None
    m = m_ref[item_id][:1]
    for t, n in enumerate(range(j, -1, -1)):
        p = jnp.exp2(s_ref[par, slot0 + t] - m).astype(BF16)
        d = _dot(vt_ref[vidx, :, n * BLK:(n + 1) * BLK], p)
        acc = d if acc is None else acc + d
        yield
    acc_ref[item_id] = acc


def _rounds(nb):
    tail = min(3, nb)
    return [(j,) for j in range(nb - 1, tail - 1, -1)] + [tuple(range(tail - 1, -1, -1))]


def _pipelined(nb, first, second, last):
    one = jnp.minimum(pl.program_id(0) + 1, 1)
    rounds = []
    for js in _rounds(nb):
        items, slot = [], 0
        for j in js:
            for half in range(2):
                items.append(((j, half), slot))
                slot += j + 1
        rounds.append(items)
    for r in range(len(rounds) + 2):
        def round_body(_, carry, r=r):
            if 0 <= r - 2 < len(rounds):
                for item, _slot in rounds[r - 2]:
                    last(item)
            gens = []
            if r < len(rounds):
                gens += [first(item, r % 2, slot0) for item, slot0 in rounds[r]]
            if 0 <= r - 1 < len(rounds):
                gens += [second(item, (r - 1) % 2, slot0) for item, slot0 in rounds[r - 1]]
            while gens:
                for g in list(gens):
                    if next(g, StopIteration) is StopIteration:
                        gens.remove(g)
            return carry

        lax.fori_loop(0, one, round_body, 0)


def _store_vt(v2, vt_ref, rows_of):
    S = v2.shape[0]
    vt = v2.astype(F32).T
    ones = jnp.ones((ONES_ROWS, S), F32)
    for idx, (lo, hi) in enumerate(rows_of):
        vt_ref[idx] = jnp.concatenate([vt[lo:hi], ones], axis=0).astype(BF16)


def _attn_scratch(S, vt_groups, vt_rows):
    nb = S // BLK
    return [
        pltpu.VMEM((2, S, LANES), BF16),
        pltpu.VMEM((vt_groups, vt_rows, S), BF16),
        pltpu.VMEM((LANES, S), F32),
        pltpu.VMEM((2 * nb, LANES, BLK), BF16),
        pltpu.VMEM((2, 2 * nb, BLK, BLK), F32),
        pltpu.VMEM((BLK, BLK), F32),
        pltpu.VMEM((2 * nb, 8, BLK), F32),
        pltpu.VMEM((2 * nb, vt_rows, BLK), F32),
    ]


def _moba_kernel(q_ref, k_ref, v_ref, qx_ref, kx_ref, o_ref,
                 ka_ref, vt_ref, qt_ref, qa_ref, s_ref, causal_ref, m_ref, acc_ref, g_ref):
    S = q_ref.shape[2]
    nb = S // BLK
    lane = lax.broadcasted_iota(jnp.int32, (1, LANES), 1)
    row = lax.broadcasted_iota(jnp.int32, (LANES, 1), 0)
    row8 = lax.broadcasted_iota(jnp.int32, (nb, BLK), 0)
    _store_causal(causal_ref)

    kf = k_ref[0, 0].astype(F32)
    qt_ref[...] = q_ref[0, 0].astype(F32).T
    for hh in range(2):
        in_head = (lane >= hh * HEAD_DIM) & (lane < (hh + 1) * HEAD_DIM)
        ka_ref[hh] = jnp.where(in_head, kf, kx_ref[hh].astype(F32)).astype(BF16)
        khf = jnp.where(in_head, kf, 0.0)
        kmean = jnp.sum(khf.reshape(nb, BLK, LANES), axis=1) * (1.0 / BLK)
        km_hi = kmean.astype(BF16)
        km_lo = (kmean - km_hi.astype(F32)).astype(BF16)
        g2 = _dot(jnp.concatenate([km_hi, km_lo], axis=0), qt_ref[...].astype(BF16))
        g_ref[hh] = g2[:nb] + g2[nb:]
    _store_vt(v_ref[0, 0], vt_ref, [(0, HEAD_DIM), (HEAD_DIM, 2 * HEAD_DIM)])

    n_sel = max(1, min(MOBA_TOPK, nb - 1))
    for j in range(nb):
        cols = slice(j * BLK, (j + 1) * BLK)
        for hh in range(2):
            ext = qx_ref[0, hh][:, j:j + 1]
            if j > 0:
                gj = g_ref[hh, :, cols]
                cnt = jnp.zeros((nb, BLK), F32)
                for m_ in range(j):
                    gm = gj[m_:m_ + 1, :]
                    beats = (gm > gj) | ((gm == gj) & (m_ < row8))
                    cnt = cnt + beats.astype(F32)
                pen = jnp.where((row8 >= j) | (cnt < n_sel), 0.0, NEG)
                off = HEAD_DIM * (1 - hh) + EXTRA_PEN
                ext = ext + jnp.concatenate(
                    [jnp.zeros((off, BLK), F32), pen, jnp.zeros((LANES - off - nb, BLK), F32)],
                    axis=0)
            head_rows = (row >= hh * HEAD_DIM) & (row < (hh + 1) * HEAD_DIM)
            qa_ref[2 * j + hh] = jnp.where(head_rows, qt_ref[:, cols], ext).astype(BF16)

    def scores(item, par, slot0):
        j, hh = item
        return _scores_t(qa_ref, ka_ref, causal_ref, s_ref, m_ref, 2 * j + hh, par, slot0, hh, j)

    def values(item, par, slot0):
        j, hh = item
        return _weighted_values_t(s_ref, m_ref, acc_ref, 2 * j + hh, par, slot0, vt_ref, hh, j)

    def finish(item):
        j, hh = item
        if hh == 1:
            halves = []
            for h2 in range(2):
                acc = acc_ref[2 * j + h2]
                halves.append(acc[:HEAD_DIM] / acc[HEAD_DIM:HEAD_DIM + 1])
            o_t = jnp.concatenate(halves, axis=0)
            o_ref[0, j * BLK:(j + 1) * BLK, :] = o_t.T.astype(BF16)

    _pipelined(nb, scores, values, finish)


def _moba(qkv, qx, kx):
    B, _, S, _ = qkv.shape
    nb = S // BLK
    n_pairs = MOBA_HEADS // 2
    blk = lambda off: pl.BlockSpec((1, 1, S, LANES), lambda b, h: (b, off + h, 0, 0))
    return pl.pallas_call(
        _moba_kernel,
        grid=(B, n_pairs),
        in_specs=[
            blk(0), blk(n_pairs), blk(2 * n_pairs),
            pl.BlockSpec((1, 2, LANES, nb), lambda b, h: (h, 0, 0, 0)),
            pl.BlockSpec((2, S, LANES), lambda b, h: (0, 0, 0)),
        ],
        out_specs=pl.BlockSpec((1, S, LANES), lambda b, h: (b, 0, h)),
        out_shape=jax.ShapeDtypeStruct((B, S, MOBA_WIDTH), BF16),
        scratch_shapes=_attn_scratch(S, 2, HEAD_DIM + ONES_ROWS) + [
            pltpu.VMEM((2, nb, S), F32),
        ],
        compiler_params=pltpu.CompilerParams(
            dimension_semantics=("arbitrary", "arbitrary"), vmem_limit_bytes=VMEM_LIMIT),
        name="moba",
    )(qkv, qkv, qkv, qx, kx)


def _diff_kernel(q_ref, k_ref, v_ref, qx_ref, kx_ref, lq1_ref, lk1_ref, lq2_ref, lk2_ref, g_ref,
                 o_ref, ka_ref, vt_ref, qt_ref, qa_ref, s_ref, causal_ref, m_ref, acc_ref, *,
                 lam_init):
    S = q_ref.shape[2]
    nb = S // BLK
    lane = lax.broadcasted_iota(jnp.int32, (1, LANES), 1)
    row = lax.broadcasted_iota(jnp.int32, (LANES, 1), 0)
    _store_causal(causal_ref)

    kf = k_ref[0, 0].astype(F32)
    qt_ref[...] = q_ref[0, 0].astype(F32).T
    for mp in range(2):
        in_map = (lane >= mp * DIFF_HEAD_DIM) & (lane < (mp + 1) * DIFF_HEAD_DIM)
        map_rows = (row >= mp * DIFF_HEAD_DIM) & (row < (mp + 1) * DIFF_HEAD_DIM)
        ka_ref[mp] = jnp.where(in_map, kf, kx_ref[mp].astype(F32)).astype(BF16)
        for j in range(nb):
            ext = qx_ref[0, mp][:, j:j + 1]
            qa_ref[2 * j + mp] = jnp.where(
                map_rows, qt_ref[:, j * BLK:(j + 1) * BLK], ext).astype(BF16)
    _store_vt(v_ref[0, 0], vt_ref, [(0, 2 * DIFF_HEAD_DIM)])

    dv = 2 * DIFF_HEAD_DIM

    def scores(item, par, slot0):
        j, mp = item
        return _scores_t(qa_ref, ka_ref, causal_ref, s_ref, m_ref, 2 * j + mp, par, slot0, mp, j)

    def values(item, par, slot0):
        j, mp = item
        return _weighted_values_t(s_ref, m_ref, acc_ref, 2 * j + mp, par, slot0, vt_ref, 0, j)

    def finish(item):
        j, mp = item
        if mp == 1:
            lam = (jnp.exp(jnp.sum(lq1_ref[...] * lk1_ref[...], axis=-1, keepdims=True))
                   - jnp.exp(jnp.sum(lq2_ref[...] * lk2_ref[...], axis=-1, keepdims=True))
                   + lam_init)
            outs = []
            for m2 in range(2):
                acc = acc_ref[2 * j + m2]
                outs.append(acc[:dv] / acc[dv:dv + 1])
            a = (outs[0] - lam * outs[1]).T
            y = _rms(a, g_ref[...], SUBLN_EPS) * (1.0 - lam_init)
            o_ref[0, j * BLK:(j + 1) * BLK, :] = y.astype(BF16)

    _pipelined(nb, scores, values, finish)


def _diff(qkv, qx, kx, lq1, lk1, lq2, lk2, subln_g, lam_init):
    B, _, S, _ = qkv.shape
    nb = S // BLK
    H = DIFF_HEADS
    base = 3 * (MOBA_HEADS // 2)
    blk = lambda off: pl.BlockSpec((1, 1, S, LANES), lambda b, h: (b, off + h, 0, 0))
    vec = lambda n: pl.BlockSpec((1, n), lambda b, h: (0, 0))
    d = DIFF_HEAD_DIM
    return pl.pallas_call(
        functools.partial(_diff_kernel, lam_init=lam_init),
        grid=(B, H),
        in_specs=[
            blk(base), blk(base + H), blk(base + 2 * H),
            pl.BlockSpec((1, 2, LANES, nb), lambda b, h: (h, 0, 0, 0)),
            pl.BlockSpec((2, S, LANES), lambda b, h: (0, 0, 0)),
            vec(d), vec(d), vec(d), vec(d), vec(2 * d),
        ],
        out_specs=pl.BlockSpec((1, S, LANES), lambda b, h: (b, 0, h)),
        out_shape=jax.ShapeDtypeStruct((B, S, DIFF_WIDTH), BF16),
        scratch_shapes=_attn_scratch(S, 1, 2 * d + ONES_ROWS),
        compiler_params=pltpu.CompilerParams(
            dimension_semantics=("arbitrary", "arbitrary"), vmem_limit_bytes=VMEM_LIMIT),
        name="diff_attn",
    )(qkv, qkv, qkv, qx, kx, lq1.reshape(1, d), lk1.reshape(1, d), lq2.reshape(1, d),
      lk2.reshape(1, d), subln_g.reshape(1, 2 * d))


def _merge_kernel(x_ref, ya_ref, yb_ref, gpre_ref, wg_ref, wa_ref, wb_ref, wo_ref, gpost_ref,
                  o_ref):
    x = x_ref[...]
    D = x.shape[-1]
    h = _rms(x, gpre_ref[...], NORM_EPS).astype(BF16)
    ga = _dot(h, wg_ref[:, :D])
    a = _dot(ya_ref[...], wa_ref[...])
    merged = jax.nn.sigmoid(ga) * a
    gb = _dot(h, wg_ref[:, D:])
    b = _dot(yb_ref[...], wb_ref[...])
    merged = merged + jax.nn.sigmoid(gb) * b
    o = _dot(merged.astype(BF16), wo_ref[...])
    o_ref[...] = x + _rms(o, gpost_ref[...], NORM_EPS)


def _const_spec(shape):
    return pl.BlockSpec(shape, lambda i: (0,) * len(shape), pipeline_mode=pl.Buffered(1))


def _merge(x2, ya2, yb2, gpre, w_g, wa, wb, wo, gpost, tm=512):
    T, D = x2.shape
    return pl.pallas_call(
        _merge_kernel,
        grid=(T // tm,),
        in_specs=[
            pl.BlockSpec((tm, D), lambda i: (i, 0)),
            pl.BlockSpec((tm, MOBA_WIDTH), lambda i: (i, 0)),
            pl.BlockSpec((tm, DIFF_WIDTH), lambda i: (i, 0)),
            _const_spec((1, D)),
            _const_spec((D, 2 * D)),
            _const_spec((MOBA_WIDTH, D)),
            _const_spec((DIFF_WIDTH, D)),
            _const_spec((D, D)),
            _const_spec((1, D)),
        ],
        out_specs=pl.BlockSpec((tm, D), lambda i: (i, 0)),
        out_shape=jax.ShapeDtypeStruct((T, D), F32),
        compiler_params=pltpu.CompilerParams(
            dimension_semantics=("arbitrary",), vmem_limit_bytes=VMEM_LIMIT),
        name="merge",
    )(x2, ya2, yb2, gpre.reshape(1, D), w_g, wa, wb, wo, gpost.reshape(1, D))


def _ffn_kernel(x_ref, gpre_ref, wg_ref, wu_ref, wd_ref, gpost_ref, o_ref, a_ref, *, chunk):
    x = x_ref[...]
    h = _rms(x, gpre_ref[...], NORM_EPS).astype(BF16)
    for c in range(D_FF // chunk):
        cols = slice(c * chunk, (c + 1) * chunk)
        g = _dot(h, wg_ref[:, cols])
        u = _dot(h, wu_ref[:, cols])
        a_ref[:, cols] = (jax.nn.silu(g) * u).astype(BF16)
    f = _dot(a_ref[...], wd_ref[...])
    o_ref[...] = x + _rms(f, gpost_ref[...], NORM_EPS)


def _ffn(x2, gpre, wg, wu, wd, gpost, tm=512):
    T, D = x2.shape
    return pl.pallas_call(
        functools.partial(_ffn_kernel, chunk=D_FF // 2),
        grid=(T // tm,),
        in_specs=[
            pl.BlockSpec((tm, D), lambda i: (i, 0)),
            _const_spec((1, D)),
            _const_spec((D, D_FF)),
            _const_spec((D, D_FF)),
            _const_spec((D_FF, D)),
            _const_spec((1, D)),
        ],
        out_specs=pl.BlockSpec((tm, D), lambda i: (i, 0)),
        out_shape=jax.ShapeDtypeStruct((T, D), F32),
        scratch_shapes=[pltpu.VMEM((tm, D_FF), BF16)],
        compiler_params=pltpu.CompilerParams(
            dimension_semantics=("arbitrary",), vmem_limit_bytes=VMEM_LIMIT),
        name="ffn",
    )(x2, gpre.reshape(1, D), wg, wu, wd, gpost.reshape(1, D))


def kernel(x, norm_mix_pre_g, w_in, w_branch_a, w_branch_b, lam_q1, lam_k1, lam_q2, lam_k2,
           diff_subln_g, w_out, norm_mix_post_g, norm_ffn_pre_g, w_gate, w_up, w_down,
           norm_ffn_post_g):
    B, S, D = x.shape
    assert D == D_MODEL and S % BLK == 0
    nb = S // BLK
    moba_slopes, diff_slopes = _alibi_slopes()
    qx_m, kx = _attn_tables(moba_slopes, S)
    qx_d, _ = _attn_tables(diff_slopes, S)
    qx_m = qx_m.reshape(MOBA_HEADS // 2, 2, 2, LANES, nb)[:, np.arange(2), np.arange(2)]
    qx_m, qx_d = jnp.asarray(qx_m), jnp.asarray(qx_d)
    kx = jnp.asarray(kx, BF16)
    depth = w_in.shape[0]
    for l in range(depth):
        lam_init = 0.8 - 0.6 * math.exp(-0.3 * l)
        w_qkv = w_in[l, :, :QKV_COLS].astype(BF16)
        w_g = w_in[l, :, QKV_COLS:].astype(BF16)
        qkv = _in_proj(x, norm_mix_pre_g[l], w_qkv)
        ya = _moba(qkv, qx_m, kx)
        yb = _diff(qkv, qx_d, kx, lam_q1[l], lam_k1[l], lam_q2[l], lam_k2[l],
                   diff_subln_g[l], lam_init)
        x2 = _merge(x.reshape(B * S, D), ya.reshape(B * S, MOBA_WIDTH),
                    yb.reshape(B * S, DIFF_WIDTH), norm_mix_pre_g[l], w_g,
                    w_branch_a[l].astype(BF16), w_branch_b[l].astype(BF16),
                    w_out[l].astype(BF16), norm_mix_post_g[l])
        x2 = _ffn(x2, norm_ffn_pre_g[l], w_gate[l].astype(BF16), w_up[l].astype(BF16),
                  w_down[l].astype(BF16), norm_ffn_post_g[l])
        x = x2.reshape(B, S, D)
    return x
```

```python
import functools
import math

import numpy as np
import jax
import jax.numpy as jnp
from jax import lax
from jax.experimental import pallas as pl
from jax.experimental.pallas import tpu as pltpu

D_MODEL = 1024
MOBA_HEADS = 8
MOBA_TOPK = 3
MOBA_WIDTH = 512
DIFF_HEADS = 4
DIFF_HEAD_DIM = 64
DIFF_WIDTH = 512
HEAD_DIM = 64
QKV_COLS = 3 * MOBA_WIDTH + 3 * DIFF_WIDTH
D_FF = 2816
NORM_EPS = 1e-6
SUBLN_EPS = 1e-5
ALIBI_MAX_BIAS = 8.0
NEG = -1e30
LOG2E = math.log2(math.e)
Q_SCALE = HEAD_DIM ** -0.5 * LOG2E

LANES = 128
BLK = 256
N_COL_TILES = QKV_COLS // LANES
ONES_ROWS = 16
EXTRA_PEN = 16
VMEM_LIMIT = 56 * 1024 * 1024

BF16 = jnp.bfloat16
F32 = jnp.float32


def _alibi_slopes():
    n = MOBA_HEADS + DIFF_HEADS
    slopes = 2.0 ** (-ALIBI_MAX_BIAS * np.arange(1, n + 1) / n)
    stride = n // DIFF_HEADS
    diff_idx = np.arange(DIFF_HEADS) * stride + (stride - 1)
    moba_idx = np.setdiff1d(np.arange(n), diff_idx)
    return (np.asarray(slopes[moba_idx], np.float32), np.asarray(slopes[diff_idx], np.float32))


def _split3(v):
    parts, rem = [], float(v)
    for _ in range(3):
        p = float(np.asarray(rem, np.float32).astype(BF16).astype(np.float64))
        parts.append(p)
        rem -= p
    return parts


def _attn_tables(slopes, S):
    nb = S // BLK
    H = len(slopes)
    qx = np.zeros((H, 2, nb, LANES), np.float32)
    kx = np.zeros((2, S, LANES), np.float32)
    pos = np.arange(S)
    for half in range(2):
        e0 = HEAD_DIM * (1 - half)
        for c in range(3):
            kx[half, :, e0 + c] = pos % BLK
            kx[half, :, e0 + 3 + c] = (pos // BLK) * BLK
            kx[half, :, e0 + 6 + c] = 1.0
        for n in range(nb):
            kx[half, n * BLK:(n + 1) * BLK, e0 + EXTRA_PEN + n] = 1.0
        for h in range(H):
            s = float(np.float32(slopes[h])) * LOG2E
            sp = _split3(s)
            for j in range(nb):
                cp = _split3(-s * j * BLK)
                for c in range(3):
                    qx[h, half, j, e0 + c] = sp[c]
                    qx[h, half, j, e0 + 3 + c] = sp[c]
                    qx[h, half, j, e0 + 6 + c] = cp[c]
    return qx.transpose(0, 1, 3, 2), kx


def _dot(a, b):
    return jnp.dot(a, b, preferred_element_type=F32)


def _rms(x, g, eps):
    return x * lax.rsqrt(jnp.mean(x * x, axis=-1, keepdims=True) + eps) * g


def _in_proj_kernel(x_ref, g_ref, w_ref, k_out, t_out, *, chunk):
    h = _rms(x_ref[0], g_ref[...], NORM_EPS).astype(BF16)
    tiles_per_chunk = chunk // LANES
    for c in range(QKV_COLS // chunk):
        acc = _dot(h, w_ref[:, c * chunk:(c + 1) * chunk])
        kind = ("q", "k", "v")[c % 3]
        for t in range(tiles_per_chunk):
            val = acc[:, t * LANES:(t + 1) * LANES]
            if kind == "k":
                k_out[0, (c // 3) * tiles_per_chunk + t] = val.astype(BF16)
            else:
                if kind == "q":
                    val = val * Q_SCALE
                slot = (2 * (c // 3) + (kind == "v")) * tiles_per_chunk + t
                t_out[0, slot] = val.astype(BF16).T


def _in_proj(x, g, w_qkv, tm=512):
    B, S, D = x.shape
    n_k = N_COL_TILES // 3
    return pl.pallas_call(
        functools.partial(_in_proj_kernel, chunk=4 * LANES),
        grid=(B, S // tm),
        in_specs=[
            pl.BlockSpec((1, tm, D), lambda b, i: (b, i, 0)),
            pl.BlockSpec((1, D), lambda b, i: (0, 0)),
            pl.BlockSpec((D, QKV_COLS), lambda b, i: (0, 0)),
        ],
        out_specs=[
            pl.BlockSpec((1, n_k, tm, LANES), lambda b, i: (b, 0, i, 0)),
            pl.BlockSpec((1, 2 * n_k, LANES, tm), lambda b, i: (b, 0, 0, i)),
        ],
        out_shape=[
            jax.ShapeDtypeStruct((B, n_k, S, LANES), BF16),
            jax.ShapeDtypeStruct((B, 2 * n_k, LANES, S), BF16),
        ],
        compiler_params=pltpu.CompilerParams(
            dimension_semantics=("arbitrary", "arbitrary"), vmem_limit_bytes=VMEM_LIMIT),
        name="in_proj",
    )(x, g.reshape(1, D), w_qkv)


def _causal_tile():
    r = np.arange(BLK)
    return np.where(r[:, None] <= r[None, :], 0.0, NEG).astype(np.float32)


_CAUSAL_SPEC = pl.BlockSpec((BLK, BLK), lambda b, h: (0, 0))


def _scores_t(qa_ref, ka_ref, causal_ref, s_ref, m_ref, item_id, par, slot0, idx, j):
    m8 = None
    for t, n in enumerate(range(j, -1, -1)):
        s = _dot(ka_ref[idx, n * BLK:(n + 1) * BLK, :], qa_ref[item_id])
        if n == j:
            s = s + causal_ref[...]
        s_ref[par, slot0 + t] = s
        t8 = jnp.max(s.reshape(BLK // 8, 8, BLK), axis=0)
        m8 = t8 if m8 is None else jnp.maximum(m8, t8)
        yield
    m_ref[item_id] = jnp.broadcast_to(jnp.max(m8, axis=0, keepdims=True), (8, BLK))


def _weighted_values_t(s_ref, m_ref, acc_ref, item_id, par, slot0, vt_ref, vidx, j):
    acc = None
    m = m_ref[item_id][:1]
    for t, n in enumerate(range(j, -1, -1)):
        p = jnp.exp2(s_ref[par, slot0 + t] - m).astype(BF16)
        d = _dot(vt_ref[vidx, :, n * BLK:(n + 1) * BLK], p)
        acc = d if acc is None else acc + d
        yield
    acc_ref[item_id] = acc


def _rounds(nb):
    tail = min(3, nb)
    return [(j,) for j in range(nb - 1, tail - 1, -1)] + [tuple(range(tail - 1, -1, -1))]


def _round_tiles(nb):
    return max(sum(2 * (j + 1) for j in js) for js in _rounds(nb))


def _pipelined(nb, first, second, last):
    one = jnp.minimum(pl.program_id(0) + 1, 1)
    rounds = []
    for js in _rounds(nb):
        items, slot = [], 0
        for j in js:
            for half in range(2):
                items.append(((j, half), slot))
                slot += j + 1
        rounds.append(items)
    for r in range(len(rounds) + 2):
        def round_body(_, carry, r=r):
            if 0 <= r - 2 < len(rounds):
                for item, _slot in rounds[r - 2]:
                    last(item)
            gens = []
            if r < len(rounds):
                gens += [first(item, r % 2, slot0) for item, slot0 in rounds[r]]
            if 0 <= r - 1 < len(rounds):
                gens += [second(item, (r - 1) % 2, slot0) for item, slot0 in rounds[r - 1]]
            while gens:
                for g in list(gens):
                    if next(g, StopIteration) is StopIteration:
                        gens.remove(g)
            return carry

        lax.fori_loop(0, one, round_body, 0)


def _store_vt(vt_in_ref, vt_ref, rows_of):
    S = vt_in_ref.shape[3]
    ones = jnp.ones((ONES_ROWS, S), BF16)
    for idx, (lo, hi) in enumerate(rows_of):
        vt_ref[idx] = jnp.concatenate([vt_in_ref[0, 0, lo:hi, :], ones], axis=0)


def _attn_scratch(S, vt_groups, vt_rows):
    nb = S // BLK
    return [
        pltpu.VMEM((2, S, LANES), BF16),
        pltpu.VMEM((vt_groups, vt_rows, S), BF16),
        pltpu.VMEM((2 * nb, LANES, BLK), BF16),
        pltpu.VMEM((2, _round_tiles(nb), BLK, BLK), F32),
        pltpu.VMEM((2 * nb, 8, BLK), F32),
        pltpu.VMEM((2 * nb, vt_rows, BLK), F32),
    ]


def _moba_kernel(qt_ref, k_ref, vt_in_ref, qx_ref, kx_ref, causal_ref, o_ref,
                 ka_ref, vt_ref, qa_ref, s_ref, m_ref, acc_ref, g_ref):
    S = k_ref.shape[2]
    qt_ref = qt_ref.at[0, 0]
    nb = S // BLK
    lane = lax.broadcasted_iota(jnp.int32, (1, LANES), 1)
    row = lax.broadcasted_iota(jnp.int32, (LANES, 1), 0)
    row8 = lax.broadcasted_iota(jnp.int32, (nb, BLK), 0)

    blk_of_key = lax.broadcasted_iota(jnp.int32, (2 * nb, S), 1) // BLK
    indicator = (blk_of_key == lax.broadcasted_iota(jnp.int32, (2 * nb, S), 0)).astype(BF16)
    kmean = _dot(indicator, k_ref[0, 0])[:nb] * (1.0 / BLK)
    for hh in range(2):
        in_head = (lane >= hh * HEAD_DIM) & (lane < (hh + 1) * HEAD_DIM)
        ka_ref[hh] = jnp.where(in_head, k_ref[0, 0], kx_ref[hh])
        km = jnp.where(in_head, kmean, 0.0)
        km_hi = km.astype(BF16)
        km_lo = (km - km_hi.astype(F32)).astype(BF16)
        g2 = _dot(jnp.concatenate([km_hi, km_lo], axis=0), qt_ref[...])
        g_ref[hh] = g2[:nb] + g2[nb:]
    _store_vt(vt_in_ref, vt_ref, [(0, HEAD_DIM), (HEAD_DIM, 2 * HEAD_DIM)])

    n_sel = max(1, min(MOBA_TOPK, nb - 1))
    for j in range(nb):
        cols = slice(j * BLK, (j + 1) * BLK)
        for hh in range(2):
            ext = qx_ref[0, hh][:, j:j + 1]
            if j > 0:
                gj = g_ref[hh, :, cols]
                cnt = jnp.zeros((nb, BLK), F32)
                for m_ in range(j):
                    gm = gj[m_:m_ + 1, :]
                    beats = (gm > gj) | ((gm == gj) & (m_ < row8))
                    cnt = cnt + beats.astype(F32)
                pen = jnp.where((row8 >= j) | (cnt < n_sel), 0.0, NEG)
                off = HEAD_DIM * (1 - hh) + EXTRA_PEN
                ext = ext + jnp.concatenate(
                    [jnp.zeros((off, BLK), F32), pen, jnp.zeros((LANES - off - nb, BLK), F32)],
                    axis=0)
            head_rows = (row >= hh * HEAD_DIM) & (row < (hh + 1) * HEAD_DIM)
            ext = jnp.broadcast_to(ext, (LANES, BLK)).astype(BF16)
            qa_ref[2 * j + hh] = jnp.where(head_rows, qt_ref[:, cols], ext)

    def scores(item, par, slot0):
        j, hh = item
        return _scores_t(qa_ref, ka_ref, causal_ref, s_ref, m_ref, 2 * j + hh, par, slot0, hh, j)

    def values(item, par, slot0):
        j, hh = item
        return _weighted_values_t(s_ref, m_ref, acc_ref, 2 * j + hh, par, slot0, vt_ref, hh, j)

    def finish(item):
        j, hh = item
        if hh == 1:
            halves = []
            for h2 in range(2):
                acc = acc_ref[2 * j + h2]
                halves.append(acc[:HEAD_DIM] / acc[HEAD_DIM:HEAD_DIM + 1])
            o_t = jnp.concatenate(halves, axis=0)
            o_ref[0, j * BLK:(j + 1) * BLK, :] = o_t.T.astype(BF16)

    _pipelined(nb, scores, values, finish)


def _key_spec(S, off):
    return pl.BlockSpec((1, 1, S, LANES), lambda b, h: (b, off + h, 0, 0))


def _transposed_spec(S, off):
    return pl.BlockSpec((1, 1, LANES, S), lambda b, h: (b, off + h, 0, 0))


def _moba(keys, qv_t, qx, kx):
    B, _, S, _ = keys.shape
    nb = S // BLK
    n_pairs = MOBA_HEADS // 2
    return pl.pallas_call(
        _moba_kernel,
        grid=(B, n_pairs),
        in_specs=[
            _transposed_spec(S, 0), _key_spec(S, 0), _transposed_spec(S, n_pairs),
            pl.BlockSpec((1, 2, LANES, nb), lambda b, h: (h, 0, 0, 0)),
            pl.BlockSpec((2, S, LANES), lambda b, h: (0, 0, 0)),
            _CAUSAL_SPEC,
        ],
        out_specs=pl.BlockSpec((1, S, LANES), lambda b, h: (b, 0, h)),
        out_shape=jax.ShapeDtypeStruct((B, S, MOBA_WIDTH), BF16),
        scratch_shapes=_attn_scratch(S, 2, HEAD_DIM + ONES_ROWS) + [
            pltpu.VMEM((2, nb, S), F32),
        ],
        compiler_params=pltpu.CompilerParams(
            dimension_semantics=("arbitrary", "arbitrary"), vmem_limit_bytes=VMEM_LIMIT),
        name="moba",
    )(qv_t, keys, qv_t, qx, kx, jnp.asarray(_causal_tile()))


def _diff_kernel(qt_ref, k_ref, vt_in_ref, qx_ref, kx_ref, causal_ref, lq1_ref, lk1_ref, lq2_ref,
                 lk2_ref, g_ref, o_ref, ka_ref, vt_ref, qa_ref, s_ref, m_ref, acc_ref, *,
                 lam_init):
    S = k_ref.shape[2]
    qt_ref = qt_ref.at[0, 0]
    nb = S // BLK
    lane = lax.broadcasted_iota(jnp.int32, (1, LANES), 1)
    row = lax.broadcasted_iota(jnp.int32, (LANES, 1), 0)

    for mp in range(2):
        in_map = (lane >= mp * DIFF_HEAD_DIM) & (lane < (mp + 1) * DIFF_HEAD_DIM)
        map_rows = (row >= mp * DIFF_HEAD_DIM) & (row < (mp + 1) * DIFF_HEAD_DIM)
        ka_ref[mp] = jnp.where(in_map, k_ref[0, 0], kx_ref[mp])
        for j in range(nb):
            ext = qx_ref[0, mp][:, j:j + 1]
            ext = jnp.broadcast_to(ext, (LANES, BLK)).astype(BF16)
            qa_ref[2 * j + mp] = jnp.where(map_rows, qt_ref[:, j * BLK:(j + 1) * BLK], ext)
    _store_vt(vt_in_ref, vt_ref, [(0, 2 * DIFF_HEAD_DIM)])

    dv = 2 * DIFF_HEAD_DIM

    def scores(item, par, slot0):
        j, mp = item
        return _scores_t(qa_ref, ka_ref, causal_ref, s_ref, m_ref, 2 * j + mp, par, slot0, mp, j)

    def values(item, par, slot0):
        j, mp = item
        return _weighted_values_t(s_ref, m_ref, acc_ref, 2 * j + mp, par, slot0, vt_ref, 0, j)

    def finish(item):
        j, mp = item
        if mp == 1:
            lam = (jnp.exp(jnp.sum(lq1_ref[...] * lk1_ref[...], axis=-1, keepdims=True))
                   - jnp.exp(jnp.sum(lq2_ref[...] * lk2_ref[...], axis=-1, keepdims=True))
                   + lam_init)
            outs = []
            for m2 in range(2):
                acc = acc_ref[2 * j + m2]
                outs.append(acc[:dv] / acc[dv:dv + 1])
            a = (outs[0] - lam * outs[1]).T
            y = _rms(a, g_ref[...], SUBLN_EPS) * (1.0 - lam_init)
            o_ref[0, j * BLK:(j + 1) * BLK, :] = y.astype(BF16)

    _pipelined(nb, scores, values, finish)


def _diff(keys, qv_t, qx, kx, lq1, lk1, lq2, lk2, subln_g, lam_init):
    B, _, S, _ = keys.shape
    nb = S // BLK
    H = DIFF_HEADS
    n_pairs = MOBA_HEADS // 2
    vec = lambda n: pl.BlockSpec((1, n), lambda b, h: (0, 0))
    d = DIFF_HEAD_DIM
    return pl.pallas_call(
        functools.partial(_diff_kernel, lam_init=lam_init),
        grid=(B, H),
        in_specs=[
            _transposed_spec(S, 2 * n_pairs), _key_spec(S, n_pairs),
            _transposed_spec(S, 2 * n_pairs + H),
            pl.BlockSpec((1, 2, LANES, nb), lambda b, h: (h, 0, 0, 0)),
            pl.BlockSpec((2, S, LANES), lambda b, h: (0, 0, 0)),
            _CAUSAL_SPEC,
            vec(d), vec(d), vec(d), vec(d), vec(2 * d),
        ],
        out_specs=pl.BlockSpec((1, S, LANES), lambda b, h: (b, 0, h)),
        out_shape=jax.ShapeDtypeStruct((B, S, DIFF_WIDTH), BF16),
        scratch_shapes=_attn_scratch(S, 1, 2 * d + ONES_ROWS),
        compiler_params=pltpu.CompilerParams(
            dimension_semantics=("arbitrary", "arbitrary"), vmem_limit_bytes=VMEM_LIMIT),
        name="diff_attn",
    )(qv_t, keys, qv_t, qx, kx, jnp.asarray(_causal_tile()), lq1.reshape(1, d),
      lk1.reshape(1, d), lq2.reshape(1, d), lk2.reshape(1, d), subln_g.reshape(1, 2 * d))


def _merge_kernel(x_ref, ya_ref, yb_ref, gpre_ref, wg_ref, wa_ref, wb_ref, wo_ref, gpost_ref,
                  o_ref):
    x = x_ref[...]
    D = x.shape[-1]
    h = _rms(x, gpre_ref[...], NORM_EPS).astype(BF16)
    ga = _dot(h, wg_ref[:, :D])
    a = _dot(ya_ref[...], wa_ref[...])
    merged = jax.nn.sigmoid(ga) * a
    gb = _dot(h, wg_ref[:, D:])
    b = _dot(yb_ref[...], wb_ref[...])
    merged = merged + jax.nn.sigmoid(gb) * b
    o = _dot(merged.astype(BF16), wo_ref[...])
    o_ref[...] = x + _rms(o, gpost_ref[...], NORM_EPS)


def _const_spec(shape):
    return pl.BlockSpec(shape, lambda i: (0,) * len(shape), pipeline_mode=pl.Buffered(1))


def _merge(x2, ya2, yb2, gpre, w_g, wa, wb, wo, gpost, tm=512):
    T, D = x2.shape
    return pl.pallas_call(
        _merge_kernel,
        grid=(T // tm,),
        in_specs=[
            pl.BlockSpec((tm, D), lambda i: (i, 0)),
            pl.BlockSpec((tm, MOBA_WIDTH), lambda i: (i, 0)),
            pl.BlockSpec((tm, DIFF_WIDTH), lambda i: (i, 0)),
            _const_spec((1, D)),
            _const_spec((D, 2 * D)),
            _const_spec((MOBA_WIDTH, D)),
            _const_spec((DIFF_WIDTH, D)),
            _const_spec((D, D)),
            _const_spec((1, D)),
        ],
        out_specs=pl.BlockSpec((tm, D), lambda i: (i, 0)),
        out_shape=jax.ShapeDtypeStruct((T, D), F32),
        compiler_params=pltpu.CompilerParams(
            dimension_semantics=("arbitrary",), vmem_limit_bytes=VMEM_LIMIT),
        name="merge",
    )(x2, ya2, yb2, gpre.reshape(1, D), w_g, wa, wb, wo, gpost.reshape(1, D))


def _ffn_kernel(x_ref, gpre_ref, wg_ref, wu_ref, wd_ref, gpost_ref, o_ref, a_ref, *, chunk):
    x = x_ref[...]
    h = _rms(x, gpre_ref[...], NORM_EPS).astype(BF16)
    for c in range(D_FF // chunk):
        cols = slice(c * chunk, (c + 1) * chunk)
        g = _dot(h, wg_ref[:, cols])
        u = _dot(h, wu_ref[:, cols])
        a_ref[:, cols] = (jax.nn.silu(g) * u).astype(BF16)
    f = _dot(a_ref[...], wd_ref[...])
    o_ref[...] = x + _rms(f, gpost_ref[...], NORM_EPS)


def _ffn(x2, gpre, wg, wu, wd, gpost, tm=512):
    T, D = x2.shape
    return pl.pallas_call(
        functools.partial(_ffn_kernel, chunk=D_FF // 2),
        grid=(T // tm,),
        in_specs=[
            pl.BlockSpec((tm, D), lambda i: (i, 0)),
            _const_spec((1, D)),
            _const_spec((D, D_FF)),
            _const_spec((D, D_FF)),
            _const_spec((D_FF, D)),
            _const_spec((1, D)),
        ],
        out_specs=pl.BlockSpec((tm, D), lambda i: (i, 0)),
        out_shape=jax.ShapeDtypeStruct((T, D), F32),
        scratch_shapes=[pltpu.VMEM((tm, D_FF), BF16)],
        compiler_params=pltpu.CompilerParams(
            dimension_semantics=("arbitrary",), vmem_limit_bytes=VMEM_LIMIT),
        name="ffn",
    )(x2, gpre.reshape(1, D), wg, wu, wd, gpost.reshape(1, D))


def kernel(x, norm_mix_pre_g, w_in, w_branch_a, w_branch_b, lam_q1, lam_k1, lam_q2, lam_k2,
           diff_subln_g, w_out, norm_mix_post_g, norm_ffn_pre_g, w_gate, w_up, w_down,
           norm_ffn_post_g):
    B, S, D = x.shape
    assert D == D_MODEL and S % BLK == 0
    nb = S // BLK
    moba_slopes, diff_slopes = _alibi_slopes()
    qx_m, kx = _attn_tables(moba_slopes, S)
    qx_d, _ = _attn_tables(diff_slopes, S)
    qx_m = qx_m.reshape(MOBA_HEADS // 2, 2, 2, LANES, nb)[:, np.arange(2), np.arange(2)]
    qx_m, qx_d = jnp.asarray(qx_m), jnp.asarray(qx_d)
    kx = jnp.asarray(kx, BF16)
    depth = w_in.shape[0]
    for l in range(depth):
        lam_init = 0.8 - 0.6 * math.exp(-0.3 * l)
        w_qkv = w_in[l, :, :QKV_COLS].astype(BF16)
        w_g = w_in[l, :, QKV_COLS:].astype(BF16)
        keys, qv_t = _in_proj(x, norm_mix_pre_g[l], w_qkv)
        ya = _moba(keys, qv_t, qx_m, kx)
        yb = _diff(keys, qv_t, qx_d, kx, lam_q1[l], lam_k1[l], lam_q2[l], lam_k2[l],
                   diff_subln_g[l], lam_init)
        x2 = _merge(x.reshape(B * S, D), ya.reshape(B * S, MOBA_WIDTH),
                    yb.reshape(B * S, DIFF_WIDTH), norm_mix_pre_g[l], w_g,
                    w_branch_a[l].astype(BF16), w_branch_b[l].astype(BF16),
                    w_out[l].astype(BF16), norm_mix_post_g[l])
        x2 = _ffn(x2, norm_ffn_pre_g[l], w_gate[l].astype(BF16), w_up[l].astype(BF16),
                  w_down[l].astype(BF16), norm_ffn_post_g[l])
        x = x2.reshape(B, S, D)
    return x
```

```python
import functools
import math

import numpy as np
import jax
import jax.numpy as jnp
from jax import lax
from jax.experimental import pallas as pl
from jax.experimental.pallas import tpu as pltpu

D_MODEL = 1024
MOBA_HEADS = 8
MOBA_TOPK = 3
MOBA_WIDTH = 512
DIFF_HEADS = 4
DIFF_HEAD_DIM = 64
DIFF_WIDTH = 512
HEAD_DIM = 64
QKV_COLS = 3 * MOBA_WIDTH + 3 * DIFF_WIDTH
D_FF = 2816
NORM_EPS = 1e-6
SUBLN_EPS = 1e-5
ALIBI_MAX_BIAS = 8.0
NEG = -1e30
LOG2E = math.log2(math.e)
Q_SCALE = HEAD_DIM ** -0.5 * LOG2E

LANES = 128
BLK = 256
N_COL_TILES = QKV_COLS // LANES
ONES_ROWS = 16
EXTRA_PEN = 16
VMEM_LIMIT = 56 * 1024 * 1024

BF16 = jnp.bfloat16
F32 = jnp.float32


def _alibi_slopes():
    n = MOBA_HEADS + DIFF_HEADS
    slopes = 2.0 ** (-ALIBI_MAX_BIAS * np.arange(1, n + 1) / n)
    stride = n // DIFF_HEADS
    diff_idx = np.arange(DIFF_HEADS) * stride + (stride - 1)
    moba_idx = np.setdiff1d(np.arange(n), diff_idx)
    return (np.asarray(slopes[moba_idx], np.float32), np.asarray(slopes[diff_idx], np.float32))


def _split3(v):
    parts, rem = [], float(v)
    for _ in range(3):
        p = float(np.asarray(rem, np.float32).astype(BF16).astype(np.float64))
        parts.append(p)
        rem -= p
    return parts


def _attn_tables(slopes, S):
    nb = S // BLK
    H = len(slopes)
    qx = np.zeros((H, 2, nb, LANES), np.float32)
    kx = np.zeros((2, S, LANES), np.float32)
    pos = np.arange(S)
    for half in range(2):
        e0 = HEAD_DIM * (1 - half)
        for c in range(3):
            kx[half, :, e0 + c] = pos % BLK
            kx[half, :, e0 + 3 + c] = (pos // BLK) * BLK
            kx[half, :, e0 + 6 + c] = 1.0
        for n in range(nb):
            kx[half, n * BLK:(n + 1) * BLK, e0 + EXTRA_PEN + n] = 1.0
        for h in range(H):
            s = float(np.float32(slopes[h])) * LOG2E
            sp = _split3(s)
            for j in range(nb):
                cp = _split3(-s * j * BLK)
                for c in range(3):
                    qx[h, half, j, e0 + c] = sp[c]
                    qx[h, half, j, e0 + 3 + c] = sp[c]
                    qx[h, half, j, e0 + 6 + c] = cp[c]
    return qx.transpose(0, 1, 3, 2), kx


def _dot(a, b):
    return jnp.dot(a, b, preferred_element_type=F32)


def _rms(x, g, eps):
    return x * lax.rsqrt(jnp.mean(x * x, axis=-1, keepdims=True) + eps) * g


def _in_proj_kernel(x_ref, g_ref, w_ref, k_out, t_out, *, chunk, parts):
    tm = x_ref.shape[1]
    rows = [slice(p * tm // parts, (p + 1) * tm // parts) for p in range(parts)]
    hs = [_rms(x_ref[0, r, :], g_ref[...], NORM_EPS).astype(BF16) for r in rows]
    tiles_per_chunk = chunk // LANES
    for r, h in zip(rows, hs):
        for c in range(QKV_COLS // chunk):
            acc = _dot(h, w_ref[:, c * chunk:(c + 1) * chunk])
            kind = ("q", "k", "v")[c % 3]
            for t in range(tiles_per_chunk):
                val = acc[:, t * LANES:(t + 1) * LANES]
                if kind == "k":
                    k_out[0, (c // 3) * tiles_per_chunk + t, r, :] = val.astype(BF16)
                else:
                    if kind == "q":
                        val = val * Q_SCALE
                    slot = (2 * (c // 3) + (kind == "v")) * tiles_per_chunk + t
                    t_out[0, slot, :, r] = val.astype(BF16).T


def _in_proj(x, g, w_qkv, tm=1024):
    B, S, D = x.shape
    n_k = N_COL_TILES // 3
    return pl.pallas_call(
        functools.partial(_in_proj_kernel, chunk=4 * LANES, parts=4),
        grid=(B, S // tm),
        in_specs=[
            pl.BlockSpec((1, tm, D), lambda b, i: (b, i, 0)),
            pl.BlockSpec((1, D), lambda b, i: (0, 0)),
            pl.BlockSpec((D, QKV_COLS), lambda b, i: (0, 0)),
        ],
        out_specs=[
            pl.BlockSpec((1, n_k, tm, LANES), lambda b, i: (b, 0, i, 0)),
            pl.BlockSpec((1, 2 * n_k, LANES, tm), lambda b, i: (b, 0, 0, i)),
        ],
        out_shape=[
            jax.ShapeDtypeStruct((B, n_k, S, LANES), BF16),
            jax.ShapeDtypeStruct((B, 2 * n_k, LANES, S), BF16),
        ],
        compiler_params=pltpu.CompilerParams(
            dimension_semantics=("arbitrary", "arbitrary"), vmem_limit_bytes=VMEM_LIMIT),
        name="in_proj",
    )(x, g.reshape(1, D), w_qkv)


def _causal_tile():
    r = np.arange(BLK)
    return np.where(r[:, None] <= r[None, :], 0.0, NEG).astype(np.float32)


_CAUSAL_SPEC = pl.BlockSpec((BLK, BLK), lambda b, h: (0, 0))


def _scores_t(qa_ref, ka_ref, causal_ref, s_ref, m_ref, item_id, par, slot0, idx, j):
    m8 = None
    for t, n in enumerate(range(j, -1, -1)):
        s = _dot(ka_ref[idx, n * BLK:(n + 1) * BLK, :], qa_ref[item_id])
        if n == j:
            s = s + causal_ref[...]
        s_ref[par, slot0 + t] = s
        t8 = jnp.max(s.reshape(BLK // 8, 8, BLK), axis=0)
        m8 = t8 if m8 is None else jnp.maximum(m8, t8)
        yield
    m_ref[item_id] = jnp.broadcast_to(jnp.max(m8, axis=0, keepdims=True), (8, BLK))


def _weighted_values_t(s_ref, m_ref, acc_ref, item_id, par, slot0, vt_ref, vidx, j):
    acc = None
    m = m_ref[item_id][:1]
    for t, n in enumerate(range(j, -1, -1)):
        p = jnp.exp2(s_ref[par, slot0 + t] - m).astype(BF16)
        d = _dot(vt_ref[vidx, :, n * BLK:(n + 1) * BLK], p)
        acc = d if acc is None else acc + d
        yield
    acc_ref[item_id] = acc


def _rounds(nb):
    tail = min(3, nb)
    return [(j,) for j in range(nb - 1, tail - 1, -1)] + [tuple(range(tail - 1, -1, -1))]


def _round_tiles(nb):
    return max(sum(2 * (j + 1) for j in js) for js in _rounds(nb))


def _pipelined(nb, first, second, last):
    one = jnp.minimum(pl.program_id(0) + 1, 1)
    rounds = []
    for js in _rounds(nb):
        items, slot = [], 0
        for j in js:
            for half in range(2):
                items.append(((j, half), slot))
                slot += j + 1
        rounds.append(items)
    for r in range(len(rounds) + 2):
        def round_body(_, carry, r=r):
            if 0 <= r - 2 < len(rounds):
                for item, _slot in rounds[r - 2]:
                    last(item)
            gens = []
            if r < len(rounds):
                gens += [first(item, r % 2, slot0) for item, slot0 in rounds[r]]
            if 0 <= r - 1 < len(rounds):
                gens += [second(item, (r - 1) % 2, slot0) for item, slot0 in rounds[r - 1]]
            while gens:
                for g in list(gens):
                    if next(g, StopIteration) is StopIteration:
                        gens.remove(g)
            return carry

        lax.fori_loop(0, one, round_body, 0)


def _store_vt(vt_in_ref, vt_ref, rows_of):
    S = vt_in_ref.shape[3]
    ones = jnp.ones((ONES_ROWS, S), BF16)
    for idx, (lo, hi) in enumerate(rows_of):
        vt_ref[idx] = jnp.concatenate([vt_in_ref[0, 0, lo:hi, :], ones], axis=0)


def _attn_scratch(S, vt_groups, vt_rows):
    nb = S // BLK
    return [
        pltpu.VMEM((2, S, LANES), BF16),
        pltpu.VMEM((vt_groups, vt_rows, S), BF16),
        pltpu.VMEM((2 * nb, LANES, BLK), BF16),
        pltpu.VMEM((2, _round_tiles(nb), BLK, BLK), F32),
        pltpu.VMEM((2 * nb, 8, BLK), F32),
        pltpu.VMEM((2 * nb, vt_rows, BLK), F32),
    ]


def _moba_kernel(qt_ref, k_ref, vt_in_ref, qx_ref, kx_ref, causal_ref, o_ref,
                 ka_ref, vt_ref, qa_ref, s_ref, m_ref, acc_ref, g_ref):
    S = k_ref.shape[2]
    qt_ref = qt_ref.at[0, 0]
    nb = S // BLK
    lane = lax.broadcasted_iota(jnp.int32, (1, LANES), 1)
    row = lax.broadcasted_iota(jnp.int32, (LANES, 1), 0)
    row8 = lax.broadcasted_iota(jnp.int32, (nb, BLK), 0)

    blk_of_key = lax.broadcasted_iota(jnp.int32, (2 * nb, S), 1) // BLK
    indicator = (blk_of_key == lax.broadcasted_iota(jnp.int32, (2 * nb, S), 0)).astype(BF16)
    kmean = _dot(indicator, k_ref[0, 0])[:nb] * (1.0 / BLK)
    for hh in range(2):
        in_head = (lane >= hh * HEAD_DIM) & (lane < (hh + 1) * HEAD_DIM)
        ka_ref[hh] = jnp.where(in_head, k_ref[0, 0], kx_ref[hh])
        km = jnp.where(in_head, kmean, 0.0)
        km_hi = km.astype(BF16)
        km_lo = (km - km_hi.astype(F32)).astype(BF16)
        g2 = _dot(jnp.concatenate([km_hi, km_lo], axis=0), qt_ref[...])
        g_ref[hh] = g2[:nb] + g2[nb:]
    _store_vt(vt_in_ref, vt_ref, [(0, HEAD_DIM), (HEAD_DIM, 2 * HEAD_DIM)])

    n_sel = max(1, min(MOBA_TOPK, nb - 1))
    for j in range(nb):
        cols = slice(j * BLK, (j + 1) * BLK)
        for hh in range(2):
            ext = qx_ref[0, hh][:, j:j + 1]
            if j > 0:
                gj = g_ref[hh, :, cols]
                cnt = jnp.zeros((nb, BLK), F32)
                for m_ in range(j):
                    gm = gj[m_:m_ + 1, :]
                    beats = (gm > gj) | ((gm == gj) & (m_ < row8))
                    cnt = cnt + beats.astype(F32)
                pen = jnp.where((row8 >= j) | (cnt < n_sel), 0.0, NEG)
                off = HEAD_DIM * (1 - hh) + EXTRA_PEN
                ext = ext + jnp.concatenate(
                    [jnp.zeros((off, BLK), F32), pen, jnp.zeros((LANES - off - nb, BLK), F32)],
                    axis=0)
            head_rows = (row >= hh * HEAD_DIM) & (row < (hh + 1) * HEAD_DIM)
            ext = jnp.broadcast_to(ext, (LANES, BLK)).astype(BF16)
            qa_ref[2 * j + hh] = jnp.where(head_rows, qt_ref[:, cols], ext)

    def scores(item, par, slot0):
        j, hh = item
        return _scores_t(qa_ref, ka_ref, causal_ref, s_ref, m_ref, 2 * j + hh, par, slot0, hh, j)

    def values(item, par, slot0):
        j, hh = item
        return _weighted_values_t(s_ref, m_ref, acc_ref, 2 * j + hh, par, slot0, vt_ref, hh, j)

    def finish(item):
        j, hh = item
        if hh == 1:
            halves = []
            for h2 in range(2):
                acc = acc_ref[2 * j + h2]
                halves.append(acc[:HEAD_DIM] / acc[HEAD_DIM:HEAD_DIM + 1])
            o_t = jnp.concatenate(halves, axis=0)
            o_ref[0, j * BLK:(j + 1) * BLK, :] = o_t.T.astype(BF16)

    _pipelined(nb, scores, values, finish)


def _key_spec(S, off):
    return pl.BlockSpec((1, 1, S, LANES), lambda b, h: (b, off + h, 0, 0))


def _transposed_spec(S, off):
    return pl.BlockSpec((1, 1, LANES, S), lambda b, h: (b, off + h, 0, 0))


def _moba(keys, qv_t, qx, kx):
    B, _, S, _ = keys.shape
    nb = S // BLK
    n_pairs = MOBA_HEADS // 2
    return pl.pallas_call(
        _moba_kernel,
        grid=(B, n_pairs),
        in_specs=[
            _transposed_spec(S, 0), _key_spec(S, 0), _transposed_spec(S, n_pairs),
            pl.BlockSpec((1, 2, LANES, nb), lambda b, h: (h, 0, 0, 0)),
            pl.BlockSpec((2, S, LANES), lambda b, h: (0, 0, 0)),
            _CAUSAL_SPEC,
        ],
        out_specs=pl.BlockSpec((1, S, LANES), lambda b, h: (b, 0, h)),
        out_shape=jax.ShapeDtypeStruct((B, S, MOBA_WIDTH), BF16),
        scratch_shapes=_attn_scratch(S, 2, HEAD_DIM + ONES_ROWS) + [
            pltpu.VMEM((2, nb, S), F32),
        ],
        compiler_params=pltpu.CompilerParams(
            dimension_semantics=("arbitrary", "arbitrary"), vmem_limit_bytes=VMEM_LIMIT),
        name="moba",
    )(qv_t, keys, qv_t, qx, kx, jnp.asarray(_causal_tile()))


def _diff_kernel(qt_ref, k_ref, vt_in_ref, qx_ref, kx_ref, causal_ref, lq1_ref, lk1_ref, lq2_ref,
                 lk2_ref, g_ref, o_ref, ka_ref, vt_ref, qa_ref, s_ref, m_ref, acc_ref, *,
                 lam_init):
    S = k_ref.shape[2]
    qt_ref = qt_ref.at[0, 0]
    nb = S // BLK
    lane = lax.broadcasted_iota(jnp.int32, (1, LANES), 1)
    row = lax.broadcasted_iota(jnp.int32, (LANES, 1), 0)

    for mp in range(2):
        in_map = (lane >= mp * DIFF_HEAD_DIM) & (lane < (mp + 1) * DIFF_HEAD_DIM)
        map_rows = (row >= mp * DIFF_HEAD_DIM) & (row < (mp + 1) * DIFF_HEAD_DIM)
        ka_ref[mp] = jnp.where(in_map, k_ref[0, 0], kx_ref[mp])
        for j in range(nb):
            ext = qx_ref[0, mp][:, j:j + 1]
            ext = jnp.broadcast_to(ext, (LANES, BLK)).astype(BF16)
            qa_ref[2 * j + mp] = jnp.where(map_rows, qt_ref[:, j * BLK:(j + 1) * BLK], ext)
    _store_vt(vt_in_ref, vt_ref, [(0, 2 * DIFF_HEAD_DIM)])

    dv = 2 * DIFF_HEAD_DIM

    def scores(item, par, slot0):
        j, mp = item
        return _scores_t(qa_ref, ka_ref, causal_ref, s_ref, m_ref, 2 * j + mp, par, slot0, mp, j)

    def values(item, par, slot0):
        j, mp = item
        return _weighted_values_t(s_ref, m_ref, acc_ref, 2 * j + mp, par, slot0, vt_ref, 0, j)

    def finish(item):
        j, mp = item
        if mp == 1:
            lam = (jnp.exp(jnp.sum(lq1_ref[...] * lk1_ref[...], axis=-1, keepdims=True))
                   - jnp.exp(jnp.sum(lq2_ref[...] * lk2_ref[...], axis=-1, keepdims=True))
                   + lam_init)
            outs = []
            for m2 in range(2):
                acc = acc_ref[2 * j + m2]
                outs.append(acc[:dv] / acc[dv:dv + 1])
            a = (outs[0] - lam * outs[1]).T
            y = _rms(a, g_ref[...], SUBLN_EPS) * (1.0 - lam_init)
            o_ref[0, j * BLK:(j + 1) * BLK, :] = y.astype(BF16)

    _pipelined(nb, scores, values, finish)


def _diff(keys, qv_t, qx, kx, lq1, lk1, lq2, lk2, subln_g, lam_init):
    B, _, S, _ = keys.shape
    nb = S // BLK
    H = DIFF_HEADS
    n_pairs = MOBA_HEADS // 2
    vec = lambda n: pl.BlockSpec((1, n), lambda b, h: (0, 0))
    d = DIFF_HEAD_DIM
    return pl.pallas_call(
        functools.partial(_diff_kernel, lam_init=lam_init),
        grid=(B, H),
        in_specs=[
            _transposed_spec(S, 2 * n_pairs), _key_spec(S, n_pairs),
            _transposed_spec(S, 2 * n_pairs + H),
            pl.BlockSpec((1, 2, LANES, nb), lambda b, h: (h, 0, 0, 0)),
            pl.BlockSpec((2, S, LANES), lambda b, h: (0, 0, 0)),
            _CAUSAL_SPEC,
            vec(d), vec(d), vec(d), vec(d), vec(2 * d),
        ],
        out_specs=pl.BlockSpec((1, S, LANES), lambda b, h: (b, 0, h)),
        out_shape=jax.ShapeDtypeStruct((B, S, DIFF_WIDTH), BF16),
        scratch_shapes=_attn_scratch(S, 1, 2 * d + ONES_ROWS),
        compiler_params=pltpu.CompilerParams(
            dimension_semantics=("arbitrary", "arbitrary"), vmem_limit_bytes=VMEM_LIMIT),
        name="diff_attn",
    )(qv_t, keys, qv_t, qx, kx, jnp.asarray(_causal_tile()), lq1.reshape(1, d),
      lk1.reshape(1, d), lq2.reshape(1, d), lk2.reshape(1, d), subln_g.reshape(1, 2 * d))


def _merge_kernel(x_ref, ya_ref, yb_ref, gpre_ref, wg_ref, wa_ref, wb_ref, wo_ref, gpost_ref,
                  o_ref, *, parts):
    tm, D = x_ref.shape
    rows = [slice(p * tm // parts, (p + 1) * tm // parts) for p in range(parts)]
    hs = [_rms(x_ref[r, :], gpre_ref[...], NORM_EPS).astype(BF16) for r in rows]
    for r, h in zip(rows, hs):
        ga = _dot(h, wg_ref[:, :D])
        a = _dot(ya_ref[r, :], wa_ref[...])
        merged = jax.nn.sigmoid(ga) * a
        gb = _dot(h, wg_ref[:, D:])
        b = _dot(yb_ref[r, :], wb_ref[...])
        merged = merged + jax.nn.sigmoid(gb) * b
        o = _dot(merged.astype(BF16), wo_ref[...])
        o_ref[r, :] = x_ref[r, :] + _rms(o, gpost_ref[...], NORM_EPS)


def _const_spec(shape):
    return pl.BlockSpec(shape, lambda i: (0,) * len(shape), pipeline_mode=pl.Buffered(1))


def _merge(x2, ya2, yb2, gpre, w_g, wa, wb, wo, gpost, tm=1024):
    T, D = x2.shape
    return pl.pallas_call(
        functools.partial(_merge_kernel, parts=4),
        grid=(T // tm,),
        in_specs=[
            pl.BlockSpec((tm, D), lambda i: (i, 0)),
            pl.BlockSpec((tm, MOBA_WIDTH), lambda i: (i, 0)),
            pl.BlockSpec((tm, DIFF_WIDTH), lambda i: (i, 0)),
            _const_spec((1, D)),
            _const_spec((D, 2 * D)),
            _const_spec((MOBA_WIDTH, D)),
            _const_spec((DIFF_WIDTH, D)),
            _const_spec((D, D)),
            _const_spec((1, D)),
        ],
        out_specs=pl.BlockSpec((tm, D), lambda i: (i, 0)),
        out_shape=jax.ShapeDtypeStruct((T, D), F32),
        compiler_params=pltpu.CompilerParams(
            dimension_semantics=("arbitrary",), vmem_limit_bytes=VMEM_LIMIT),
        name="merge",
    )(x2, ya2, yb2, gpre.reshape(1, D), w_g, wa, wb, wo, gpost.reshape(1, D))


def _ffn_kernel(x_ref, gpre_ref, wg_ref, wu_ref, wd_ref, gpost_ref, o_ref, a_ref, *, chunk, parts):
    tm = x_ref.shape[0]
    rows = [slice(p * tm // parts, (p + 1) * tm // parts) for p in range(parts)]
    hs = [_rms(x_ref[r, :], gpre_ref[...], NORM_EPS).astype(BF16) for r in rows]
    for c in range(D_FF // chunk):
        cols = slice(c * chunk, (c + 1) * chunk)
        for r, h in zip(rows, hs):
            g = _dot(h, wg_ref[:, cols])
            u = _dot(h, wu_ref[:, cols])
            a_ref[r, cols] = (jax.nn.silu(g) * u).astype(BF16)
    for r in rows:
        f = _dot(a_ref[r, :], wd_ref[...])
        o_ref[r, :] = x_ref[r, :] + _rms(f, gpost_ref[...], NORM_EPS)


def _ffn(x2, gpre, wg, wu, wd, gpost, tm=1024):
    T, D = x2.shape
    return pl.pallas_call(
        functools.partial(_ffn_kernel, chunk=D_FF // 2, parts=4),
        grid=(T // tm,),
        in_specs=[
            pl.BlockSpec((tm, D), lambda i: (i, 0)),
            _const_spec((1, D)),
            _const_spec((D, D_FF)),
            _const_spec((D, D_FF)),
            _const_spec((D_FF, D)),
            _const_spec((1, D)),
        ],
        out_specs=pl.BlockSpec((tm, D), lambda i: (i, 0)),
        out_shape=jax.ShapeDtypeStruct((T, D), F32),
        scratch_shapes=[pltpu.VMEM((tm, D_FF), BF16)],
        compiler_params=pltpu.CompilerParams(
            dimension_semantics=("arbitrary",), vmem_limit_bytes=VMEM_LIMIT),
        name="ffn",
    )(x2, gpre.reshape(1, D), wg, wu, wd, gpost.reshape(1, D))


def kernel(x, norm_mix_pre_g, w_in, w_branch_a, w_branch_b, lam_q1, lam_k1, lam_q2, lam_k2,
           diff_subln_g, w_out, norm_mix_post_g, norm_ffn_pre_g, w_gate, w_up, w_down,
           norm_ffn_post_g):
    B, S, D = x.shape
    assert D == D_MODEL and S % BLK == 0
    nb = S // BLK
    moba_slopes, diff_slopes = _alibi_slopes()
    qx_m, kx = _attn_tables(moba_slopes, S)
    qx_d, _ = _attn_tables(diff_slopes, S)
    qx_m = qx_m.reshape(MOBA_HEADS // 2, 2, 2, LANES, nb)[:, np.arange(2), np.arange(2)]
    qx_m, qx_d = jnp.asarray(qx_m), jnp.asarray(qx_d)
    kx = jnp.asarray(kx, BF16)
    depth = w_in.shape[0]
    for l in range(depth):
        lam_init = 0.8 - 0.6 * math.exp(-0.3 * l)
        w_qkv = w_in[l, :, :QKV_COLS].astype(BF16)
        w_g = w_in[l, :, QKV_COLS:].astype(BF16)
        keys, qv_t = _in_proj(x, norm_mix_pre_g[l], w_qkv)
        ya = _moba(keys, qv_t, qx_m, kx)
        yb = _diff(keys, qv_t, qx_d, kx, lam_q1[l], lam_k1[l], lam_q2[l], lam_k2[l],
                   diff_subln_g[l], lam_init)
        x2 = _merge(x.reshape(B * S, D), ya.reshape(B * S, MOBA_WIDTH),
                    yb.reshape(B * S, DIFF_WIDTH), norm_mix_pre_g[l], w_g,
                    w_branch_a[l].astype(BF16), w_branch_b[l].astype(BF16),
                    w_out[l].astype(BF16), norm_mix_post_g[l])
        x2 = _ffn(x2, norm_ffn_pre_g[l], w_gate[l].astype(BF16), w_up[l].astype(BF16),
                  w_down[l].astype(BF16), norm_ffn_post_g[l])
        x = x2.reshape(B, S, D)
    return x
```

```python
import functools
import math

import numpy as np
import jax
import jax.numpy as jnp
from jax import lax
from jax.experimental import pallas as pl
from jax.experimental.pallas import tpu as pltpu

D_MODEL = 1024
MOBA_HEADS = 8
MOBA_TOPK = 3
MOBA_WIDTH = 512
DIFF_HEADS = 4
DIFF_HEAD_DIM = 64
DIFF_WIDTH = 512
HEAD_DIM = 64
QKV_COLS = 3 * MOBA_WIDTH + 3 * DIFF_WIDTH
D_FF = 2816
NORM_EPS = 1e-6
SUBLN_EPS = 1e-5
ALIBI_MAX_BIAS = 8.0
NEG = -1e30
LOG2E = math.log2(math.e)
Q_SCALE = HEAD_DIM ** -0.5 * LOG2E

LANES = 128
BLK = 256
N_COL_TILES = QKV_COLS // LANES
ONES_ROWS = 16
EXTRA_PEN = 16
VMEM_LIMIT = 56 * 1024 * 1024

BF16 = jnp.bfloat16
F32 = jnp.float32


def _alibi_slopes():
    n = MOBA_HEADS + DIFF_HEADS
    slopes = 2.0 ** (-ALIBI_MAX_BIAS * np.arange(1, n + 1) / n)
    stride = n // DIFF_HEADS
    diff_idx = np.arange(DIFF_HEADS) * stride + (stride - 1)
    moba_idx = np.setdiff1d(np.arange(n), diff_idx)
    return (np.asarray(slopes[moba_idx], np.float32), np.asarray(slopes[diff_idx], np.float32))


def _split3(v):
    parts, rem = [], float(v)
    for _ in range(3):
        p = float(np.asarray(rem, np.float32).astype(BF16).astype(np.float64))
        parts.append(p)
        rem -= p
    return parts


def _attn_tables(slopes, S):
    nb = S // BLK
    H = len(slopes)
    qx = np.zeros((H, 2, nb, LANES), np.float32)
    kx = np.zeros((2, S, LANES), np.float32)
    pos = np.arange(S)
    for half in range(2):
        e0 = HEAD_DIM * (1 - half)
        for c in range(3):
            kx[half, :, e0 + c] = pos % BLK
            kx[half, :, e0 + 3 + c] = (pos // BLK) * BLK
            kx[half, :, e0 + 6 + c] = 1.0
        for n in range(nb):
            kx[half, n * BLK:(n + 1) * BLK, e0 + EXTRA_PEN + n] = 1.0
        for h in range(H):
            s = float(np.float32(slopes[h])) * LOG2E
            sp = _split3(s)
            for j in range(nb):
                cp = _split3(-s * j * BLK)
                for c in range(3):
                    qx[h, half, j, e0 + c] = sp[c]
                    qx[h, half, j, e0 + 3 + c] = sp[c]
                    qx[h, half, j, e0 + 6 + c] = cp[c]
    return qx.transpose(0, 1, 3, 2), kx


def _dot(a, b):
    return jnp.dot(a, b, preferred_element_type=F32)


def _rms(x, g, eps):
    return x * lax.rsqrt(jnp.mean(x * x, axis=-1, keepdims=True) + eps) * g


def _in_proj_kernel(x_ref, g_ref, w_ref, k_out, t_out, *, chunk, parts):
    tm = x_ref.shape[1]
    rows = [slice(p * tm // parts, (p + 1) * tm // parts) for p in range(parts)]
    hs = [_rms(x_ref[0, r, :], g_ref[0], NORM_EPS).astype(BF16) for r in rows]
    tiles_per_chunk = chunk // LANES
    for r, h in zip(rows, hs):
        for c in range(QKV_COLS // chunk):
            acc = _dot(h, w_ref[0, :, c * chunk:(c + 1) * chunk])
            kind = ("q", "k", "v")[c % 3]
            for t in range(tiles_per_chunk):
                val = acc[:, t * LANES:(t + 1) * LANES]
                if kind == "k":
                    k_out[0, (c // 3) * tiles_per_chunk + t, r, :] = val.astype(BF16)
                else:
                    if kind == "q":
                        val = val * Q_SCALE
                    slot = (2 * (c // 3) + (kind == "v")) * tiles_per_chunk + t
                    t_out[0, slot, :, r] = val.astype(BF16).T


def _in_proj(x, g, w_in, l, tm=1024):
    B, S, D = x.shape
    n_k = N_COL_TILES // 3
    return pl.pallas_call(
        functools.partial(_in_proj_kernel, chunk=4 * LANES, parts=4),
        grid=(B, S // tm),
        in_specs=[
            pl.BlockSpec((1, tm, D), lambda b, i: (b, i, 0)),
            _layer_spec((1, D), l),
            _layer_spec((D, QKV_COLS), l),
        ],
        out_specs=[
            pl.BlockSpec((1, n_k, tm, LANES), lambda b, i: (b, 0, i, 0)),
            pl.BlockSpec((1, 2 * n_k, LANES, tm), lambda b, i: (b, 0, 0, i)),
        ],
        out_shape=[
            jax.ShapeDtypeStruct((B, n_k, S, LANES), BF16),
            jax.ShapeDtypeStruct((B, 2 * n_k, LANES, S), BF16),
        ],
        compiler_params=pltpu.CompilerParams(
            dimension_semantics=("arbitrary", "arbitrary"), vmem_limit_bytes=VMEM_LIMIT),
        name="in_proj",
    )(x, g[:, None], w_in)


def _causal_tile():
    r = np.arange(BLK)
    return np.where(r[:, None] <= r[None, :], 0.0, NEG).astype(np.float32)


_CAUSAL_SPEC = pl.BlockSpec((BLK, BLK), lambda b, h: (0, 0))


def _scores_t(qa_ref, ka_ref, causal_ref, s_ref, m_ref, item_id, par, slot0, idx, j):
    m8 = None
    for t, n in enumerate(range(j, -1, -1)):
        s = _dot(ka_ref[idx, n * BLK:(n + 1) * BLK, :], qa_ref[item_id])
        if n == j:
            s = s + causal_ref[...]
        s_ref[par, slot0 + t] = s
        t8 = jnp.max(s.reshape(BLK // 8, 8, BLK), axis=0)
        m8 = t8 if m8 is None else jnp.maximum(m8, t8)
        yield
    m_ref[item_id] = jnp.broadcast_to(jnp.max(m8, axis=0, keepdims=True), (8, BLK))


def _weighted_values_t(s_ref, m_ref, acc_ref, item_id, par, slot0, vt_ref, vidx, j):
    acc = None
    m = m_ref[item_id][:1]
    for t, n in enumerate(range(j, -1, -1)):
        p = jnp.exp2(s_ref[par, slot0 + t] - m).astype(BF16)
        d = _dot(vt_ref[vidx, :, n * BLK:(n + 1) * BLK], p)
        acc = d if acc is None else acc + d
        yield
    acc_ref[item_id] = acc


def _rounds(nb):
    tail = min(3, nb)
    return [(j,) for j in range(nb - 1, tail - 1, -1)] + [tuple(range(tail - 1, -1, -1))]


def _round_tiles(nb):
    return max(sum(2 * (j + 1) for j in js) for js in _rounds(nb))


def _pipelined(nb, first, second, last):
    one = jnp.minimum(pl.program_id(0) + 1, 1)
    rounds = []
    for js in _rounds(nb):
        items, slot = [], 0
        for j in js:
            for half in range(2):
                items.append(((j, half), slot))
                slot += j + 1
        rounds.append(items)
    for r in range(len(rounds) + 2):
        def round_body(_, carry, r=r):
            if 0 <= r - 2 < len(rounds):
                for item, _slot in rounds[r - 2]:
                    last(item)
            gens = []
            if r < len(rounds):
                gens += [first(item, r % 2, slot0) for item, slot0 in rounds[r]]
            if 0 <= r - 1 < len(rounds):
                gens += [second(item, (r - 1) % 2, slot0) for item, slot0 in rounds[r - 1]]
            while gens:
                for g in list(gens):
                    if next(g, StopIteration) is StopIteration:
                        gens.remove(g)
            return carry

        lax.fori_loop(0, one, round_body, 0)


def _store_vt(vt_in_ref, vt_ref, rows_of):
    S = vt_in_ref.shape[3]
    ones = jnp.ones((ONES_ROWS, S), BF16)
    for idx, (lo, hi) in enumerate(rows_of):
        vt_ref[idx] = jnp.concatenate([vt_in_ref[0, 0, lo:hi, :], ones], axis=0)


def _attn_scratch(S, vt_groups, vt_rows):
    nb = S // BLK
    return [
        pltpu.VMEM((2, S, LANES), BF16),
        pltpu.VMEM((vt_groups, vt_rows, S), BF16),
        pltpu.VMEM((2 * nb, LANES, BLK), BF16),
        pltpu.VMEM((2, _round_tiles(nb), BLK, BLK), F32),
        pltpu.VMEM((2 * nb, 8, BLK), F32),
        pltpu.VMEM((2 * nb, vt_rows, BLK), F32),
    ]


def _moba_kernel(qt_ref, k_ref, vt_in_ref, qx_ref, kx_ref, causal_ref, o_ref,
                 ka_ref, vt_ref, qa_ref, s_ref, m_ref, acc_ref, g_ref):
    S = k_ref.shape[2]
    qt_ref = qt_ref.at[0, 0]
    nb = S // BLK
    lane = lax.broadcasted_iota(jnp.int32, (1, LANES), 1)
    row = lax.broadcasted_iota(jnp.int32, (LANES, 1), 0)
    row8 = lax.broadcasted_iota(jnp.int32, (nb, BLK), 0)

    blk_of_key = lax.broadcasted_iota(jnp.int32, (2 * nb, S), 1) // BLK
    indicator = (blk_of_key == lax.broadcasted_iota(jnp.int32, (2 * nb, S), 0)).astype(BF16)
    kmean = _dot(indicator, k_ref[0, 0])[:nb] * (1.0 / BLK)
    for hh in range(2):
        in_head = (lane >= hh * HEAD_DIM) & (lane < (hh + 1) * HEAD_DIM)
        ka_ref[hh] = jnp.where(in_head, k_ref[0, 0], kx_ref[hh])
        km = jnp.where(in_head, kmean, 0.0)
        km_hi = km.astype(BF16)
        km_lo = (km - km_hi.astype(F32)).astype(BF16)
        g2 = _dot(jnp.concatenate([km_hi, km_lo], axis=0), qt_ref[...])
        g_ref[hh] = g2[:nb] + g2[nb:]
    _store_vt(vt_in_ref, vt_ref, [(0, HEAD_DIM), (HEAD_DIM, 2 * HEAD_DIM)])

    n_sel = max(1, min(MOBA_TOPK, nb - 1))
    for j in range(nb):
        cols = slice(j * BLK, (j + 1) * BLK)
        for hh in range(2):
            ext = qx_ref[0, hh][:, j:j + 1]
            if j > 0:
                gj = g_ref[hh, :, cols]
                cnt = jnp.zeros((nb, BLK), F32)
                for m_ in range(j):
                    gm = gj[m_:m_ + 1, :]
                    beats = (gm > gj) | ((gm == gj) & (m_ < row8))
                    cnt = cnt + beats.astype(F32)
                pen = jnp.where((row8 >= j) | (cnt < n_sel), 0.0, NEG)
                off = HEAD_DIM * (1 - hh) + EXTRA_PEN
                ext = ext + jnp.concatenate(
                    [jnp.zeros((off, BLK), F32), pen, jnp.zeros((LANES - off - nb, BLK), F32)],
                    axis=0)
            head_rows = (row >= hh * HEAD_DIM) & (row < (hh + 1) * HEAD_DIM)
            ext = jnp.broadcast_to(ext, (LANES, BLK)).astype(BF16)
            qa_ref[2 * j + hh] = jnp.where(head_rows, qt_ref[:, cols], ext)

    def scores(item, par, slot0):
        j, hh = item
        return _scores_t(qa_ref, ka_ref, causal_ref, s_ref, m_ref, 2 * j + hh, par, slot0, hh, j)

    def values(item, par, slot0):
        j, hh = item
        return _weighted_values_t(s_ref, m_ref, acc_ref, 2 * j + hh, par, slot0, vt_ref, hh, j)

    def finish(item):
        j, hh = item
        if hh == 1:
            halves = []
            for h2 in range(2):
                acc = acc_ref[2 * j + h2]
                halves.append(acc[:HEAD_DIM] / acc[HEAD_DIM:HEAD_DIM + 1])
            o_t = jnp.concatenate(halves, axis=0)
            o_ref[0, j * BLK:(j + 1) * BLK, :] = o_t.T.astype(BF16)

    _pipelined(nb, scores, values, finish)


def _key_spec(S, off):
    return pl.BlockSpec((1, 1, S, LANES), lambda b, h: (b, off + h, 0, 0))


def _transposed_spec(S, off):
    return pl.BlockSpec((1, 1, LANES, S), lambda b, h: (b, off + h, 0, 0))


def _moba(keys, qv_t, qx, kx):
    B, _, S, _ = keys.shape
    nb = S // BLK
    n_pairs = MOBA_HEADS // 2
    return pl.pallas_call(
        _moba_kernel,
        grid=(B, n_pairs),
        in_specs=[
            _transposed_spec(S, 0), _key_spec(S, 0), _transposed_spec(S, n_pairs),
            pl.BlockSpec((1, 2, LANES, nb), lambda b, h: (h, 0, 0, 0)),
            pl.BlockSpec((2, S, LANES), lambda b, h: (0, 0, 0)),
            _CAUSAL_SPEC,
        ],
        out_specs=pl.BlockSpec((1, S, LANES), lambda b, h: (b, 0, h)),
        out_shape=jax.ShapeDtypeStruct((B, S, MOBA_WIDTH), BF16),
        scratch_shapes=_attn_scratch(S, 2, HEAD_DIM + ONES_ROWS) + [
            pltpu.VMEM((2, nb, S), F32),
        ],
        compiler_params=pltpu.CompilerParams(
            dimension_semantics=("arbitrary", "arbitrary"), vmem_limit_bytes=VMEM_LIMIT),
        name="moba",
    )(qv_t, keys, qv_t, qx, kx, jnp.asarray(_causal_tile()))


def _diff_kernel(qt_ref, k_ref, vt_in_ref, qx_ref, kx_ref, causal_ref, lq1_ref, lk1_ref, lq2_ref,
                 lk2_ref, g_ref, o_ref, ka_ref, vt_ref, qa_ref, s_ref, m_ref, acc_ref, *,
                 lam_init):
    S = k_ref.shape[2]
    qt_ref = qt_ref.at[0, 0]
    nb = S // BLK
    lane = lax.broadcasted_iota(jnp.int32, (1, LANES), 1)
    row = lax.broadcasted_iota(jnp.int32, (LANES, 1), 0)

    for mp in range(2):
        in_map = (lane >= mp * DIFF_HEAD_DIM) & (lane < (mp + 1) * DIFF_HEAD_DIM)
        map_rows = (row >= mp * DIFF_HEAD_DIM) & (row < (mp + 1) * DIFF_HEAD_DIM)
        ka_ref[mp] = jnp.where(in_map, k_ref[0, 0], kx_ref[mp])
        for j in range(nb):
            ext = qx_ref[0, mp][:, j:j + 1]
            ext = jnp.broadcast_to(ext, (LANES, BLK)).astype(BF16)
            qa_ref[2 * j + mp] = jnp.where(map_rows, qt_ref[:, j * BLK:(j + 1) * BLK], ext)
    _store_vt(vt_in_ref, vt_ref, [(0, 2 * DIFF_HEAD_DIM)])

    dv = 2 * DIFF_HEAD_DIM

    def scores(item, par, slot0):
        j, mp = item
        return _scores_t(qa_ref, ka_ref, causal_ref, s_ref, m_ref, 2 * j + mp, par, slot0, mp, j)

    def values(item, par, slot0):
        j, mp = item
        return _weighted_values_t(s_ref, m_ref, acc_ref, 2 * j + mp, par, slot0, vt_ref, 0, j)

    def finish(item):
        j, mp = item
        if mp == 1:
            lam = (jnp.exp(jnp.sum(lq1_ref[...] * lk1_ref[...], axis=-1, keepdims=True))
                   - jnp.exp(jnp.sum(lq2_ref[...] * lk2_ref[...], axis=-1, keepdims=True))
                   + lam_init)
            outs = []
            for m2 in range(2):
                acc = acc_ref[2 * j + m2]
                outs.append(acc[:dv] / acc[dv:dv + 1])
            a = (outs[0] - lam * outs[1]).T
            y = _rms(a, g_ref[...], SUBLN_EPS) * (1.0 - lam_init)
            o_ref[0, j * BLK:(j + 1) * BLK, :] = y.astype(BF16)

    _pipelined(nb, scores, values, finish)


def _diff(keys, qv_t, qx, kx, lq1, lk1, lq2, lk2, subln_g, lam_init):
    B, _, S, _ = keys.shape
    nb = S // BLK
    H = DIFF_HEADS
    n_pairs = MOBA_HEADS // 2
    vec = lambda n: pl.BlockSpec((1, n), lambda b, h: (0, 0))
    d = DIFF_HEAD_DIM
    return pl.pallas_call(
        functools.partial(_diff_kernel, lam_init=lam_init),
        grid=(B, H),
        in_specs=[
            _transposed_spec(S, 2 * n_pairs), _key_spec(S, n_pairs),
            _transposed_spec(S, 2 * n_pairs + H),
            pl.BlockSpec((1, 2, LANES, nb), lambda b, h: (h, 0, 0, 0)),
            pl.BlockSpec((2, S, LANES), lambda b, h: (0, 0, 0)),
            _CAUSAL_SPEC,
            vec(d), vec(d), vec(d), vec(d), vec(2 * d),
        ],
        out_specs=pl.BlockSpec((1, S, LANES), lambda b, h: (b, 0, h)),
        out_shape=jax.ShapeDtypeStruct((B, S, DIFF_WIDTH), BF16),
        scratch_shapes=_attn_scratch(S, 1, 2 * d + ONES_ROWS),
        compiler_params=pltpu.CompilerParams(
            dimension_semantics=("arbitrary", "arbitrary"), vmem_limit_bytes=VMEM_LIMIT),
        name="diff_attn",
    )(qv_t, keys, qv_t, qx, kx, jnp.asarray(_causal_tile()), lq1.reshape(1, d),
      lk1.reshape(1, d), lq2.reshape(1, d), lk2.reshape(1, d), subln_g.reshape(1, 2 * d))


def _layer_spec(block, l, col_block=0):
    index = (l,) + (0,) * (len(block) - 1) + (col_block,)
    return pl.BlockSpec((1,) + tuple(block), lambda *_: index, pipeline_mode=pl.Buffered(1))


def _merge_kernel(x_ref, ya_ref, yb_ref, gpre_ref, wga_ref, wgb_ref, wa_ref, wb_ref, wo_ref,
                  gpost_ref, o_ref, *, parts):
    tm, D = x_ref.shape
    rows = [slice(p * tm // parts, (p + 1) * tm // parts) for p in range(parts)]
    hs = [_rms(x_ref[r, :], gpre_ref[0], NORM_EPS).astype(BF16) for r in rows]
    for r, h in zip(rows, hs):
        ga = _dot(h, wga_ref[0])
        a = _dot(ya_ref[r, :], wa_ref[0])
        merged = jax.nn.sigmoid(ga) * a
        gb = _dot(h, wgb_ref[0])
        b = _dot(yb_ref[r, :], wb_ref[0])
        merged = merged + jax.nn.sigmoid(gb) * b
        o = _dot(merged.astype(BF16), wo_ref[0])
        o_ref[r, :] = x_ref[r, :] + _rms(o, gpost_ref[0], NORM_EPS)


def _merge(x2, ya2, yb2, gpre, w_in, wa, wb, wo, gpost, l, tm=1024):
    T, D = x2.shape
    gate_col = QKV_COLS // D
    return pl.pallas_call(
        functools.partial(_merge_kernel, parts=4),
        grid=(T // tm,),
        in_specs=[
            pl.BlockSpec((tm, D), lambda i: (i, 0)),
            pl.BlockSpec((tm, MOBA_WIDTH), lambda i: (i, 0)),
            pl.BlockSpec((tm, DIFF_WIDTH), lambda i: (i, 0)),
            _layer_spec((1, D), l),
            _layer_spec((D, D), l, gate_col),
            _layer_spec((D, D), l, gate_col + 1),
            _layer_spec((MOBA_WIDTH, D), l),
            _layer_spec((DIFF_WIDTH, D), l),
            _layer_spec((D, D), l),
            _layer_spec((1, D), l),
        ],
        out_specs=pl.BlockSpec((tm, D), lambda i: (i, 0)),
        out_shape=jax.ShapeDtypeStruct((T, D), F32),
        compiler_params=pltpu.CompilerParams(
            dimension_semantics=("arbitrary",), vmem_limit_bytes=VMEM_LIMIT),
        name="merge",
    )(x2, ya2, yb2, gpre[:, None], w_in, w_in, wa, wb, wo, gpost[:, None])


def _ffn_kernel(x_ref, gpre_ref, wg_ref, wu_ref, wd_ref, gpost_ref, o_ref, a_ref, *, chunk, parts):
    tm = x_ref.shape[0]
    rows = [slice(p * tm // parts, (p + 1) * tm // parts) for p in range(parts)]
    hs = [_rms(x_ref[r, :], gpre_ref[0], NORM_EPS).astype(BF16) for r in rows]
    for c in range(D_FF // chunk):
        cols = slice(c * chunk, (c + 1) * chunk)
        for r, h in zip(rows, hs):
            g = _dot(h, wg_ref[0, :, cols])
            u = _dot(h, wu_ref[0, :, cols])
            a_ref[r, cols] = (jax.nn.silu(g) * u).astype(BF16)
    for r in rows:
        f = _dot(a_ref[r, :], wd_ref[0])
        o_ref[r, :] = x_ref[r, :] + _rms(f, gpost_ref[0], NORM_EPS)


def _ffn(x2, gpre, wg, wu, wd, gpost, l, tm=512):
    T, D = x2.shape
    return pl.pallas_call(
        functools.partial(_ffn_kernel, chunk=D_FF // 2, parts=2),
        grid=(T // tm,),
        in_specs=[
            pl.BlockSpec((tm, D), lambda i: (i, 0)),
            _layer_spec((1, D), l),
            _layer_spec((D, D_FF), l),
            _layer_spec((D, D_FF), l),
            _layer_spec((D_FF, D), l),
            _layer_spec((1, D), l),
        ],
        out_specs=pl.BlockSpec((tm, D), lambda i: (i, 0)),
        out_shape=jax.ShapeDtypeStruct((T, D), F32),
        scratch_shapes=[pltpu.VMEM((tm, D_FF), BF16)],
        compiler_params=pltpu.CompilerParams(
            dimension_semantics=("arbitrary",), vmem_limit_bytes=VMEM_LIMIT),
        name="ffn",
    )(x2, gpre[:, None], wg, wu, wd, gpost[:, None])


def kernel(x, norm_mix_pre_g, w_in, w_branch_a, w_branch_b, lam_q1, lam_k1, lam_q2, lam_k2,
           diff_subln_g, w_out, norm_mix_post_g, norm_ffn_pre_g, w_gate, w_up, w_down,
           norm_ffn_post_g):
    B, S, D = x.shape
    assert D == D_MODEL and S % BLK == 0
    nb = S // BLK
    moba_slopes, diff_slopes = _alibi_slopes()
    qx_m, kx = _attn_tables(moba_slopes, S)
    qx_d, _ = _attn_tables(diff_slopes, S)
    qx_m = qx_m.reshape(MOBA_HEADS // 2, 2, 2, LANES, nb)[:, np.arange(2), np.arange(2)]
    qx_m, qx_d = jnp.asarray(qx_m), jnp.asarray(qx_d)
    kx = jnp.asarray(kx, BF16)
    depth = w_in.shape[0]
    for l in range(depth):
        lam_init = 0.8 - 0.6 * math.exp(-0.3 * l)
        keys, qv_t = _in_proj(x, norm_mix_pre_g, w_in, l)
        ya = _moba(keys, qv_t, qx_m, kx)
        yb = _diff(keys, qv_t, qx_d, kx, lam_q1[l], lam_k1[l], lam_q2[l], lam_k2[l],
                   diff_subln_g[l], lam_init)
        x2 = _merge(x.reshape(B * S, D), ya.reshape(B * S, MOBA_WIDTH),
                    yb.reshape(B * S, DIFF_WIDTH), norm_mix_pre_g, w_in, w_branch_a,
                    w_branch_b, w_out, norm_mix_post_g, l)
        x2 = _ffn(x2, norm_ffn_pre_g, w_gate, w_up, w_down, norm_ffn_post_g, l)
        x = x2.reshape(B, S, D)
    return x
```

```python
import functools
import math

import numpy as np
import jax
import jax.numpy as jnp
from jax import lax
from jax.experimental import pallas as pl
from jax.experimental.pallas import tpu as pltpu

D_MODEL = 1024
MOBA_HEADS = 8
MOBA_TOPK = 3
MOBA_WIDTH = 512
DIFF_HEADS = 4
DIFF_HEAD_DIM = 64
DIFF_WIDTH = 512
HEAD_DIM = 64
QKV_COLS = 3 * MOBA_WIDTH + 3 * DIFF_WIDTH
D_FF = 2816
NORM_EPS = 1e-6
SUBLN_EPS = 1e-5
ALIBI_MAX_BIAS = 8.0
NEG = -1e30
LOG2E = math.log2(math.e)
Q_SCALE = HEAD_DIM ** -0.5 * LOG2E

LANES = 128
BLK = 256
N_COL_TILES = QKV_COLS // LANES
ONES_ROWS = 16
EXTRA_PEN = 16
VMEM_LIMIT = 56 * 1024 * 1024

BF16 = jnp.bfloat16
F32 = jnp.float32


def _alibi_slopes():
    n = MOBA_HEADS + DIFF_HEADS
    slopes = 2.0 ** (-ALIBI_MAX_BIAS * np.arange(1, n + 1) / n)
    stride = n // DIFF_HEADS
    diff_idx = np.arange(DIFF_HEADS) * stride + (stride - 1)
    moba_idx = np.setdiff1d(np.arange(n), diff_idx)
    return (np.asarray(slopes[moba_idx], np.float32), np.asarray(slopes[diff_idx], np.float32))


def _split3(v):
    parts, rem = [], float(v)
    for _ in range(3):
        p = float(np.asarray(rem, np.float32).astype(BF16).astype(np.float64))
        parts.append(p)
        rem -= p
    return parts


def _attn_tables(slopes, S):
    nb = S // BLK
    H = len(slopes)
    qx = np.zeros((H, 2, nb, LANES), np.float32)
    kx = np.zeros((2, S, LANES), np.float32)
    pos = np.arange(S)
    for half in range(2):
        e0 = HEAD_DIM * (1 - half)
        for c in range(3):
            kx[half, :, e0 + c] = pos % BLK
            kx[half, :, e0 + 3 + c] = (pos // BLK) * BLK
            kx[half, :, e0 + 6 + c] = 1.0
        for n in range(nb):
            kx[half, n * BLK:(n + 1) * BLK, e0 + EXTRA_PEN + n] = 1.0
        for h in range(H):
            s = float(np.float32(slopes[h])) * LOG2E
            sp = _split3(s)
            for j in range(nb):
                cp = _split3(-s * j * BLK)
                for c in range(3):
                    qx[h, half, j, e0 + c] = sp[c]
                    qx[h, half, j, e0 + 3 + c] = sp[c]
                    qx[h, half, j, e0 + 6 + c] = cp[c]
    return qx.transpose(0, 1, 3, 2), kx


def _dot(a, b):
    return jnp.dot(a, b, preferred_element_type=F32)


def _rms(x, g, eps):
    return x * lax.rsqrt(jnp.mean(x * x, axis=-1, keepdims=True) + eps) * g


def _in_proj_kernel(x_ref, g_ref, w_ref, k_out, t_out, *, chunk, parts):
    tm = x_ref.shape[1]
    rows = [slice(p * tm // parts, (p + 1) * tm // parts) for p in range(parts)]
    hs = [_rms(x_ref[0, r, :], g_ref[0], NORM_EPS).astype(BF16) for r in rows]
    tiles_per_chunk = chunk // LANES
    for r, h in zip(rows, hs):
        for c in range(QKV_COLS // chunk):
            acc = _dot(h, w_ref[0, :, c * chunk:(c + 1) * chunk])
            kind = ("q", "k", "v")[c % 3]
            for t in range(tiles_per_chunk):
                val = acc[:, t * LANES:(t + 1) * LANES]
                if kind == "k":
                    k_out[0, (c // 3) * tiles_per_chunk + t, r, :] = val.astype(BF16)
                else:
                    if kind == "q":
                        val = val * Q_SCALE
                    slot = (2 * (c // 3) + (kind == "v")) * tiles_per_chunk + t
                    t_out[0, slot, :, r] = val.astype(BF16).T


def _in_proj(x, g, w_in, l, tm=1024):
    B, S, D = x.shape
    n_k = N_COL_TILES // 3
    return pl.pallas_call(
        functools.partial(_in_proj_kernel, chunk=4 * LANES, parts=4),
        grid=(B, S // tm),
        in_specs=[
            pl.BlockSpec((1, tm, D), lambda b, i: (b, i, 0)),
            _layer_spec((1, D), l),
            _layer_spec((D, QKV_COLS), l),
        ],
        out_specs=[
            pl.BlockSpec((1, n_k, tm, LANES), lambda b, i: (b, 0, i, 0)),
            pl.BlockSpec((1, 2 * n_k, LANES, tm), lambda b, i: (b, 0, 0, i)),
        ],
        out_shape=[
            jax.ShapeDtypeStruct((B, n_k, S, LANES), BF16),
            jax.ShapeDtypeStruct((B, 2 * n_k, LANES, S), BF16),
        ],
        compiler_params=pltpu.CompilerParams(
            dimension_semantics=("arbitrary", "arbitrary"), vmem_limit_bytes=VMEM_LIMIT),
        name="in_proj",
    )(x, g[:, None], w_in)


def _causal_tile():
    r = np.arange(BLK)
    return np.where(r[:, None] <= r[None, :], 0.0, NEG).astype(np.float32)


_CAUSAL_SPEC = pl.BlockSpec((BLK, BLK), lambda b, h: (0, 0))


def _scores_t(qa_ref, ka_ref, causal_ref, s_ref, m_ref, item_id, par, slot0, idx, j):
    m8 = None
    for t, n in enumerate(range(j, -1, -1)):
        s = _dot(ka_ref[idx, n * BLK:(n + 1) * BLK, :], qa_ref[item_id])
        if n == j:
            s = s + causal_ref[...]
        s_ref[par, slot0 + t] = s
        t8 = jnp.max(s.reshape(BLK // 8, 8, BLK), axis=0)
        m8 = t8 if m8 is None else jnp.maximum(m8, t8)
        yield
    m_ref[item_id] = jnp.broadcast_to(jnp.max(m8, axis=0, keepdims=True), (8, BLK))


def _weighted_values_t(s_ref, m_ref, acc_ref, item_id, par, slot0, vt_ref, vidx, j):
    acc = None
    m = m_ref[item_id][:1]
    for t, n in enumerate(range(j, -1, -1)):
        p = jnp.exp2(s_ref[par, slot0 + t] - m).astype(BF16)
        d = _dot(vt_ref[vidx, :, n * BLK:(n + 1) * BLK], p)
        acc = d if acc is None else acc + d
        yield
    acc_ref[item_id] = acc


def _rounds(nb):
    tail = min(3, nb)
    return [(j,) for j in range(nb - 1, tail - 1, -1)] + [tuple(range(tail - 1, -1, -1))]


def _round_tiles(nb):
    return max(sum(2 * (j + 1) for j in js) for js in _rounds(nb))


def _pipelined(nb, first, second, last):
    one = jnp.minimum(pl.program_id(0) + 1, 1)
    rounds = []
    for js in _rounds(nb):
        items, slot = [], 0
        for j in js:
            for half in range(2):
                items.append(((j, half), slot))
                slot += j + 1
        rounds.append(items)
    for r in range(len(rounds) + 2):
        def round_body(_, carry, r=r):
            if 0 <= r - 2 < len(rounds):
                for item, _slot in rounds[r - 2]:
                    last(item)
            gens = []
            if r < len(rounds):
                gens += [first(item, r % 2, slot0) for item, slot0 in rounds[r]]
            if 0 <= r - 1 < len(rounds):
                gens += [second(item, (r - 1) % 2, slot0) for item, slot0 in rounds[r - 1]]
            while gens:
                for g in list(gens):
                    if next(g, StopIteration) is StopIteration:
                        gens.remove(g)
            return carry

        lax.fori_loop(0, one, round_body, 0)


def _store_vt(vt_in_ref, vt_ref, rows_of):
    S = vt_in_ref.shape[3]
    ones = jnp.ones((ONES_ROWS, S), BF16)
    for idx, (lo, hi) in enumerate(rows_of):
        vt_ref[idx] = jnp.concatenate([vt_in_ref[0, 0, lo:hi, :], ones], axis=0)


def _attn_scratch(S, vt_groups, vt_rows):
    nb = S // BLK
    return [
        pltpu.VMEM((2, S, LANES), BF16),
        pltpu.VMEM((vt_groups, vt_rows, S), BF16),
        pltpu.VMEM((2 * nb, LANES, BLK), BF16),
        pltpu.VMEM((2, _round_tiles(nb), BLK, BLK), F32),
        pltpu.VMEM((2 * nb, 8, BLK), F32),
        pltpu.VMEM((2 * nb, vt_rows, BLK), F32),
    ]


def _moba_kernel(qt_ref, k_ref, vt_in_ref, qx_ref, kx_ref, causal_ref, o_ref,
                 ka_ref, vt_ref, qa_ref, s_ref, m_ref, acc_ref, g_ref):
    S = k_ref.shape[2]
    qt_ref = qt_ref.at[0, 0]
    nb = S // BLK
    lane = lax.broadcasted_iota(jnp.int32, (1, LANES), 1)
    row = lax.broadcasted_iota(jnp.int32, (LANES, 1), 0)
    row8 = lax.broadcasted_iota(jnp.int32, (nb, BLK), 0)

    blk_of_key = lax.broadcasted_iota(jnp.int32, (2 * nb, S), 1) // BLK
    indicator = (blk_of_key == lax.broadcasted_iota(jnp.int32, (2 * nb, S), 0)).astype(BF16)
    kmean = _dot(indicator, k_ref[0, 0])[:nb] * (1.0 / BLK)
    for hh in range(2):
        in_head = (lane >= hh * HEAD_DIM) & (lane < (hh + 1) * HEAD_DIM)
        ka_ref[hh] = jnp.where(in_head, k_ref[0, 0], kx_ref[hh])
        km = jnp.where(in_head, kmean, 0.0)
        km_hi = km.astype(BF16)
        km_lo = (km - km_hi.astype(F32)).astype(BF16)
        g2 = _dot(jnp.concatenate([km_hi, km_lo], axis=0), qt_ref[...])
        g_ref[hh] = g2[:nb] + g2[nb:]
    _store_vt(vt_in_ref, vt_ref, [(0, HEAD_DIM), (HEAD_DIM, 2 * HEAD_DIM)])

    n_sel = max(1, min(MOBA_TOPK, nb - 1))
    for j in range(nb):
        cols = slice(j * BLK, (j + 1) * BLK)
        for hh in range(2):
            ext = qx_ref[0, hh][:, j:j + 1]
            if j > 0:
                gj = g_ref[hh, :, cols]
                cnt = jnp.zeros((nb, BLK), F32)
                for m_ in range(j):
                    gm = gj[m_:m_ + 1, :]
                    beats = (gm > gj) | ((gm == gj) & (m_ < row8))
                    cnt = cnt + beats.astype(F32)
                pen = jnp.where((row8 >= j) | (cnt < n_sel), 0.0, NEG)
                off = HEAD_DIM * (1 - hh) + EXTRA_PEN
                ext = ext + jnp.concatenate(
                    [jnp.zeros((off, BLK), F32), pen, jnp.zeros((LANES - off - nb, BLK), F32)],
                    axis=0)
            head_rows = (row >= hh * HEAD_DIM) & (row < (hh + 1) * HEAD_DIM)
            ext = jnp.broadcast_to(ext, (LANES, BLK)).astype(BF16)
            qa_ref[2 * j + hh] = jnp.where(head_rows, qt_ref[:, cols], ext)

    def scores(item, par, slot0):
        j, hh = item
        return _scores_t(qa_ref, ka_ref, causal_ref, s_ref, m_ref, 2 * j + hh, par, slot0, hh, j)

    def values(item, par, slot0):
        j, hh = item
        return _weighted_values_t(s_ref, m_ref, acc_ref, 2 * j + hh, par, slot0, vt_ref, hh, j)

    def finish(item):
        j, hh = item
        if hh == 1:
            halves = []
            for h2 in range(2):
                acc = acc_ref[2 * j + h2]
                halves.append(acc[:HEAD_DIM] / acc[HEAD_DIM:HEAD_DIM + 1])
            o_t = jnp.concatenate(halves, axis=0)
            o_ref[0, j * BLK:(j + 1) * BLK, :] = o_t.T.astype(BF16)

    _pipelined(nb, scores, values, finish)


def _key_spec(S, off):
    return pl.BlockSpec((1, 1, S, LANES), lambda b, h: (b, off + h, 0, 0))


def _transposed_spec(S, off):
    return pl.BlockSpec((1, 1, LANES, S), lambda b, h: (b, off + h, 0, 0))


def _moba(keys, qv_t, qx, kx):
    B, _, S, _ = keys.shape
    nb = S // BLK
    n_pairs = MOBA_HEADS // 2
    return pl.pallas_call(
        _moba_kernel,
        grid=(B, n_pairs),
        in_specs=[
            _transposed_spec(S, 0), _key_spec(S, 0), _transposed_spec(S, n_pairs),
            pl.BlockSpec((1, 2, LANES, nb), lambda b, h: (h, 0, 0, 0)),
            pl.BlockSpec((2, S, LANES), lambda b, h: (0, 0, 0)),
            _CAUSAL_SPEC,
        ],
        out_specs=pl.BlockSpec((1, S, LANES), lambda b, h: (b, 0, h)),
        out_shape=jax.ShapeDtypeStruct((B, S, MOBA_WIDTH), BF16),
        scratch_shapes=_attn_scratch(S, 2, HEAD_DIM + ONES_ROWS) + [
            pltpu.VMEM((2, nb, S), F32),
        ],
        compiler_params=pltpu.CompilerParams(
            dimension_semantics=("arbitrary", "arbitrary"), vmem_limit_bytes=VMEM_LIMIT),
        name="moba",
    )(qv_t, keys, qv_t, qx, kx, jnp.asarray(_causal_tile()))


def _diff_kernel(qt_ref, k_ref, vt_in_ref, qx_ref, kx_ref, causal_ref, lq1_ref, lk1_ref, lq2_ref,
                 lk2_ref, g_ref, o_ref, ka_ref, vt_ref, qa_ref, s_ref, m_ref, acc_ref, *,
                 lam_init):
    S = k_ref.shape[2]
    qt_ref = qt_ref.at[0, 0]
    nb = S // BLK
    lane = lax.broadcasted_iota(jnp.int32, (1, LANES), 1)
    row = lax.broadcasted_iota(jnp.int32, (LANES, 1), 0)

    for mp in range(2):
        in_map = (lane >= mp * DIFF_HEAD_DIM) & (lane < (mp + 1) * DIFF_HEAD_DIM)
        map_rows = (row >= mp * DIFF_HEAD_DIM) & (row < (mp + 1) * DIFF_HEAD_DIM)
        ka_ref[mp] = jnp.where(in_map, k_ref[0, 0], kx_ref[mp])
        for j in range(nb):
            ext = qx_ref[0, mp][:, j:j + 1]
            ext = jnp.broadcast_to(ext, (LANES, BLK)).astype(BF16)
            qa_ref[2 * j + mp] = jnp.where(map_rows, qt_ref[:, j * BLK:(j + 1) * BLK], ext)
    _store_vt(vt_in_ref, vt_ref, [(0, 2 * DIFF_HEAD_DIM)])

    dv = 2 * DIFF_HEAD_DIM

    def scores(item, par, slot0):
        j, mp = item
        return _scores_t(qa_ref, ka_ref, causal_ref, s_ref, m_ref, 2 * j + mp, par, slot0, mp, j)

    def values(item, par, slot0):
        j, mp = item
        return _weighted_values_t(s_ref, m_ref, acc_ref, 2 * j + mp, par, slot0, vt_ref, 0, j)

    def finish(item):
        j, mp = item
        if mp == 1:
            lam = (jnp.exp(jnp.sum(lq1_ref[...] * lk1_ref[...], axis=-1, keepdims=True))
                   - jnp.exp(jnp.sum(lq2_ref[...] * lk2_ref[...], axis=-1, keepdims=True))
                   + lam_init)
            outs = []
            for m2 in range(2):
                acc = acc_ref[2 * j + m2]
                outs.append(acc[:dv] / acc[dv:dv + 1])
            a = (outs[0] - lam * outs[1]).T
            y = _rms(a, g_ref[...], SUBLN_EPS) * (1.0 - lam_init)
            o_ref[0, j * BLK:(j + 1) * BLK, :] = y.astype(BF16)

    _pipelined(nb, scores, values, finish)


def _diff(keys, qv_t, qx, kx, lq1, lk1, lq2, lk2, subln_g, lam_init):
    B, _, S, _ = keys.shape
    nb = S // BLK
    H = DIFF_HEADS
    n_pairs = MOBA_HEADS // 2
    vec = lambda n: pl.BlockSpec((1, n), lambda b, h: (0, 0))
    d = DIFF_HEAD_DIM
    return pl.pallas_call(
        functools.partial(_diff_kernel, lam_init=lam_init),
        grid=(B, H),
        in_specs=[
            _transposed_spec(S, 2 * n_pairs), _key_spec(S, n_pairs),
            _transposed_spec(S, 2 * n_pairs + H),
            pl.BlockSpec((1, 2, LANES, nb), lambda b, h: (h, 0, 0, 0)),
            pl.BlockSpec((2, S, LANES), lambda b, h: (0, 0, 0)),
            _CAUSAL_SPEC,
            vec(d), vec(d), vec(d), vec(d), vec(2 * d),
        ],
        out_specs=pl.BlockSpec((1, S, LANES), lambda b, h: (b, 0, h)),
        out_shape=jax.ShapeDtypeStruct((B, S, DIFF_WIDTH), BF16),
        scratch_shapes=_attn_scratch(S, 1, 2 * d + ONES_ROWS),
        compiler_params=pltpu.CompilerParams(
            dimension_semantics=("arbitrary", "arbitrary"), vmem_limit_bytes=VMEM_LIMIT),
        name="diff_attn",
    )(qv_t, keys, qv_t, qx, kx, jnp.asarray(_causal_tile()), lq1.reshape(1, d),
      lk1.reshape(1, d), lq2.reshape(1, d), lk2.reshape(1, d), subln_g.reshape(1, 2 * d))


def _layer_spec(block, l, col_block=0):
    index = (l,) + (0,) * (len(block) - 1) + (col_block,)
    return pl.BlockSpec((1,) + tuple(block), lambda *_: index, pipeline_mode=pl.Buffered(1))


def _merge_kernel(x_ref, ya_ref, yb_ref, gpre_ref, wga_ref, wgb_ref, wa_ref, wb_ref, wo_ref,
                  gpost_ref, o_ref, *, parts):
    tm, D = x_ref.shape
    rows = [slice(p * tm // parts, (p + 1) * tm // parts) for p in range(parts)]
    hs = [_rms(x_ref[r, :], gpre_ref[0], NORM_EPS).astype(BF16) for r in rows]
    for r, h in zip(rows, hs):
        ga = _dot(h, wga_ref[0])
        a = _dot(ya_ref[r, :], wa_ref[0])
        merged = jax.nn.sigmoid(ga) * a
        gb = _dot(h, wgb_ref[0])
        b = _dot(yb_ref[r, :], wb_ref[0])
        merged = merged + jax.nn.sigmoid(gb) * b
        o = _dot(merged.astype(BF16), wo_ref[0])
        o_ref[r, :] = x_ref[r, :] + _rms(o, gpost_ref[0], NORM_EPS)


def _merge(x2, ya2, yb2, gpre, w_in, wa, wb, wo, gpost, l, tm=1024):
    T, D = x2.shape
    gate_col = QKV_COLS // D
    return pl.pallas_call(
        functools.partial(_merge_kernel, parts=4),
        grid=(T // tm,),
        in_specs=[
            pl.BlockSpec((tm, D), lambda i: (i, 0)),
            pl.BlockSpec((tm, MOBA_WIDTH), lambda i: (i, 0)),
            pl.BlockSpec((tm, DIFF_WIDTH), lambda i: (i, 0)),
            _layer_spec((1, D), l),
            _layer_spec((D, D), l, gate_col),
            _layer_spec((D, D), l, gate_col + 1),
            _layer_spec((MOBA_WIDTH, D), l),
            _layer_spec((DIFF_WIDTH, D), l),
            _layer_spec((D, D), l),
            _layer_spec((1, D), l),
        ],
        out_specs=pl.BlockSpec((tm, D), lambda i: (i, 0)),
        out_shape=jax.ShapeDtypeStruct((T, D), F32),
        compiler_params=pltpu.CompilerParams(
            dimension_semantics=("arbitrary",), vmem_limit_bytes=VMEM_LIMIT),
        name="merge",
    )(x2, ya2, yb2, gpre[:, None], w_in, w_in, wa, wb, wo, gpost[:, None])


def _ffn_kernel(x_ref, gpre_ref, wg_ref, wu_ref, wd_ref, gpost_ref, o_ref, a_ref, *, chunk, parts):
    tm = x_ref.shape[0]
    rows = [slice(p * tm // parts, (p + 1) * tm // parts) for p in range(parts)]
    hs = [_rms(x_ref[r, :], gpre_ref[0], NORM_EPS).astype(BF16) for r in rows]
    for c in range(D_FF // chunk):
        cols = slice(c * chunk, (c + 1) * chunk)
        for r, h in zip(rows, hs):
            g = _dot(h, wg_ref[0, :, cols])
            u = _dot(h, wu_ref[0, :, cols])
            a_ref[r, cols] = (jax.nn.silu(g) * u).astype(BF16)
    for r in rows:
        f = _dot(a_ref[r, :], wd_ref[0])
        o_ref[r, :] = x_ref[r, :] + _rms(f, gpost_ref[0], NORM_EPS)


def _ffn(x2, gpre, wg, wu, wd, gpost, l, tm=1024):
    T, D = x2.shape
    return pl.pallas_call(
        functools.partial(_ffn_kernel, chunk=D_FF // 2, parts=4),
        grid=(T // tm,),
        in_specs=[
            pl.BlockSpec((tm, D), lambda i: (i, 0)),
            _layer_spec((1, D), l),
            _layer_spec((D, D_FF), l),
            _layer_spec((D, D_FF), l),
            _layer_spec((D_FF, D), l),
            _layer_spec((1, D), l),
        ],
        out_specs=pl.BlockSpec((tm, D), lambda i: (i, 0)),
        out_shape=jax.ShapeDtypeStruct((T, D), F32),
        scratch_shapes=[pltpu.VMEM((tm, D_FF), BF16)],
        compiler_params=pltpu.CompilerParams(
            dimension_semantics=("arbitrary",), vmem_limit_bytes=VMEM_LIMIT),
        name="ffn",
    )(x2, gpre[:, None], wg, wu, wd, gpost[:, None])


def kernel(x, norm_mix_pre_g, w_in, w_branch_a, w_branch_b, lam_q1, lam_k1, lam_q2, lam_k2,
           diff_subln_g, w_out, norm_mix_post_g, norm_ffn_pre_g, w_gate, w_up, w_down,
           norm_ffn_post_g):
    B, S, D = x.shape
    assert D == D_MODEL and S % BLK == 0
    nb = S // BLK
    moba_slopes, diff_slopes = _alibi_slopes()
    qx_m, kx = _attn_tables(moba_slopes, S)
    qx_d, _ = _attn_tables(diff_slopes, S)
    qx_m = qx_m.reshape(MOBA_HEADS // 2, 2, 2, LANES, nb)[:, np.arange(2), np.arange(2)]
    qx_m, qx_d = jnp.asarray(qx_m), jnp.asarray(qx_d)
    kx = jnp.asarray(kx, BF16)
    depth = w_in.shape[0]
    for l in range(depth):
        lam_init = 0.8 - 0.6 * math.exp(-0.3 * l)
        keys, qv_t = _in_proj(x, norm_mix_pre_g, w_in, l)
        ya = _moba(keys, qv_t, qx_m, kx)
        yb = _diff(keys, qv_t, qx_d, kx, lam_q1[l], lam_k1[l], lam_q2[l], lam_k2[l],
                   diff_subln_g[l], lam_init)
        x2 = _merge(x.reshape(B * S, D), ya.reshape(B * S, MOBA_WIDTH),
                    yb.reshape(B * S, DIFF_WIDTH), norm_mix_pre_g, w_in, w_branch_a,
                    w_branch_b, w_out, norm_mix_post_g, l)
        x2 = _ffn(x2, norm_ffn_pre_g, w_gate.astype(BF16), w_up.astype(BF16),
                  w_down.astype(BF16), norm_ffn_post_g, l)
        x = x2.reshape(B, S, D)
    return x
```

```python
import functools
import math

import numpy as np
import jax
import jax.numpy as jnp
from jax import lax
from jax.experimental import pallas as pl
from jax.experimental.pallas import tpu as pltpu

D_MODEL = 1024
MOBA_HEADS = 8
MOBA_TOPK = 3
MOBA_WIDTH = 512
DIFF_HEADS = 4
DIFF_HEAD_DIM = 64
DIFF_WIDTH = 512
HEAD_DIM = 64
QKV_COLS = 3 * MOBA_WIDTH + 3 * DIFF_WIDTH
D_FF = 2816
NORM_EPS = 1e-6
SUBLN_EPS = 1e-5
ALIBI_MAX_BIAS = 8.0
NEG = -1e30
LOG2E = math.log2(math.e)
Q_SCALE = HEAD_DIM ** -0.5 * LOG2E

LANES = 128
BLK = 256
N_COL_TILES = QKV_COLS // LANES
N_QK_TILES = N_COL_TILES // 3
N_PAIRS = MOBA_HEADS // 2
ONES_ROWS = 16
EXTRA_PEN = 16
VMEM_LIMIT = 56 * 1024 * 1024

BF16 = jnp.bfloat16
F32 = jnp.float32


def _alibi_slopes():
    n = MOBA_HEADS + DIFF_HEADS
    slopes = 2.0 ** (-ALIBI_MAX_BIAS * np.arange(1, n + 1) / n)
    stride = n // DIFF_HEADS
    diff_idx = np.arange(DIFF_HEADS) * stride + (stride - 1)
    moba_idx = np.setdiff1d(np.arange(n), diff_idx)
    return (np.asarray(slopes[moba_idx], np.float32), np.asarray(slopes[diff_idx], np.float32))


def _split3(v):
    parts, rem = [], float(v)
    for _ in range(3):
        p = float(np.asarray(rem, np.float32).astype(BF16).astype(np.float64))
        parts.append(p)
        rem -= p
    return parts


def _attn_tables(slopes, S):
    nb = S // BLK
    H = len(slopes)
    qx = np.zeros((H, 2, nb, LANES), np.float32)
    kx = np.zeros((2, S, LANES), np.float32)
    pos = np.arange(S)
    for half in range(2):
        e0 = HEAD_DIM * (1 - half)
        for c in range(3):
            kx[half, :, e0 + c] = pos % BLK
            kx[half, :, e0 + 3 + c] = (pos // BLK) * BLK
            kx[half, :, e0 + 6 + c] = 1.0
        for n in range(nb):
            kx[half, n * BLK:(n + 1) * BLK, e0 + EXTRA_PEN + n] = 1.0
        for h in range(H):
            s = float(np.float32(slopes[h])) * LOG2E
            sp = _split3(s)
            for j in range(nb):
                cp = _split3(-s * j * BLK)
                for c in range(3):
                    qx[h, half, j, e0 + c] = sp[c]
                    qx[h, half, j, e0 + 3 + c] = sp[c]
                    qx[h, half, j, e0 + 6 + c] = cp[c]
    return qx.transpose(0, 1, 3, 2), kx


def _query_extras(S, per_step):
    nb = S // BLK
    moba_slopes, diff_slopes = _alibi_slopes()
    qx_m, _ = _attn_tables(moba_slopes, S)
    qx_d, _ = _attn_tables(diff_slopes, S)
    qx_m = qx_m.reshape(N_PAIRS, 2, 2, LANES, nb)[:, np.arange(2), np.arange(2)]
    qx = np.concatenate([qx_m, qx_d], axis=0)
    qx = qx.reshape(N_QK_TILES, 2, LANES, nb // per_step, per_step)
    return np.ascontiguousarray(qx.transpose(3, 0, 1, 2, 4))


def _dot(a, b):
    return jnp.dot(a, b, preferred_element_type=F32)


def _rms(x, g, eps):
    return x * lax.rsqrt(jnp.mean(x * x, axis=-1, keepdims=True) + eps) * g


def _layer_spec(block, l, col_block=0):
    index = (l,) + (0,) * (len(block) - 1) + (col_block,)
    return pl.BlockSpec((1,) + tuple(block), lambda *_: index, pipeline_mode=pl.Buffered(1))


def _in_proj_kernel(x_ref, g_ref, w_ref, kx_ref, qx_ref, ka_out, qa_out, vtm_out, vtd_out,
                    ksum_out, *, parts):
    tm = x_ref.shape[1]
    part = tm // parts
    assert part == BLK
    chunk = 4 * LANES
    lane = lax.broadcasted_iota(jnp.int32, (1, LANES), 1)
    row = lax.broadcasted_iota(jnp.int32, (LANES, 1), 0)
    in_half = [(lane >= h * HEAD_DIM) & (lane < (h + 1) * HEAD_DIM) for h in range(2)]
    half_rows = [(row >= h * HEAD_DIM) & (row < (h + 1) * HEAD_DIM) for h in range(2)]
    ones = jnp.ones((ONES_ROWS, part), BF16)
    rows = [slice(p * part, (p + 1) * part) for p in range(parts)]
    hs = [_rms(x_ref[0, r, :], g_ref[0], NORM_EPS).astype(BF16) for r in rows]
    for p, (r, h) in enumerate(zip(rows, hs)):
        for c in range(QKV_COLS // chunk):
            acc = _dot(h, w_ref[0, :, c * chunk:(c + 1) * chunk])
            kind, mixer = ("q", "k", "v")[c % 3], c // 3
            for t in range(4):
                tile = 4 * mixer + t
                val = acc[:, t * LANES:(t + 1) * LANES]
                if kind == "k":
                    if mixer == 0:
                        ksum_out[0, 0, t, p:p + 1, :] = jnp.sum(val, axis=0, keepdims=True)
                    val = val.astype(BF16)
                    for half in range(2):
                        ka_out[0, tile, half, r, :] = jnp.where(
                            in_half[half], val, kx_ref[half, r, :])
                elif kind == "q":
                    val_t = (val * Q_SCALE).astype(BF16).T
                    for half in range(2):
                        ext = jnp.broadcast_to(qx_ref[0, tile, half][:, p:p + 1], (LANES, part))
                        qa_out[0, tile, half, :, r] = jnp.where(
                            half_rows[half], val_t, ext.astype(BF16))
                else:
                    val_t = val.astype(BF16).T
                    if mixer == 0:
                        for hh in range(2):
                            vtm_out[0, t, hh, :, r] = jnp.concatenate(
                                [val_t[hh * HEAD_DIM:(hh + 1) * HEAD_DIM], ones], axis=0)
                    else:
                        vtd_out[0, t, :, r] = jnp.concatenate([val_t, ones], axis=0)


def _in_proj(x, g, w_in, l, tm=1024):
    B, S, D = x.shape
    parts = tm // BLK
    kx = jnp.asarray(_attn_tables(_alibi_slopes()[0], S)[1], BF16)
    qx = jnp.asarray(_query_extras(S, parts))
    n, dv = N_QK_TILES, 2 * DIFF_HEAD_DIM
    return pl.pallas_call(
        functools.partial(_in_proj_kernel, parts=parts),
        grid=(B, S // tm),
        in_specs=[
            pl.BlockSpec((1, tm, D), lambda b, i: (b, i, 0)),
            _layer_spec((1, D), l),
            _layer_spec((D, QKV_COLS), l),
            pl.BlockSpec((2, tm, LANES), lambda b, i: (0, i, 0)),
            pl.BlockSpec((1, n, 2, LANES, parts), lambda b, i: (i, 0, 0, 0, 0)),
        ],
        out_specs=[
            pl.BlockSpec((1, n, 2, tm, LANES), lambda b, i: (b, 0, 0, i, 0)),
            pl.BlockSpec((1, n, 2, LANES, tm), lambda b, i: (b, 0, 0, 0, i)),
            pl.BlockSpec((1, N_PAIRS, 2, HEAD_DIM + ONES_ROWS, tm), lambda b, i: (b, 0, 0, 0, i)),
            pl.BlockSpec((1, DIFF_HEADS, dv + ONES_ROWS, tm), lambda b, i: (b, 0, 0, i)),
            pl.BlockSpec((1, 1, N_PAIRS, parts, LANES), lambda b, i: (b, i, 0, 0, 0)),
        ],
        out_shape=[
            jax.ShapeDtypeStruct((B, n, 2, S, LANES), BF16),
            jax.ShapeDtypeStruct((B, n, 2, LANES, S), BF16),
            jax.ShapeDtypeStruct((B, N_PAIRS, 2, HEAD_DIM + ONES_ROWS, S), BF16),
            jax.ShapeDtypeStruct((B, DIFF_HEADS, dv + ONES_ROWS, S), BF16),
            jax.ShapeDtypeStruct((B, S // tm, N_PAIRS, parts, LANES), F32),
        ],
        compiler_params=pltpu.CompilerParams(
            dimension_semantics=("arbitrary", "arbitrary"), vmem_limit_bytes=VMEM_LIMIT),
        name="in_proj",
    )(x, g[:, None], w_in, kx, qx)


def _causal_tile():
    r = np.arange(BLK)
    return np.where(r[:, None] <= r[None, :], 0.0, NEG).astype(np.float32)


_CAUSAL_SPEC = pl.BlockSpec((BLK, BLK), lambda b, h: (0, 0))


def _scores_t(qa_tile, ka_ref, causal_ref, s_ref, m_ref, item_id, par, slot0, j):
    m8 = None
    for t, n in enumerate(range(j, -1, -1)):
        s = _dot(ka_ref[n * BLK:(n + 1) * BLK, :], qa_tile[...])
        if n == j:
            s = s + causal_ref[...]
        s_ref[par, slot0 + t] = s
        t8 = jnp.max(s.reshape(BLK // 8, 8, BLK), axis=0)
        m8 = t8 if m8 is None else jnp.maximum(m8, t8)
        yield
    m_ref[item_id] = jnp.broadcast_to(jnp.max(m8, axis=0, keepdims=True), (8, BLK))


def _weighted_values_t(s_ref, m_ref, acc_ref, item_id, par, slot0, vt_ref, j):
    acc = None
    m = m_ref[item_id][:1]
    for t, n in enumerate(range(j, -1, -1)):
        p = jnp.exp2(s_ref[par, slot0 + t] - m).astype(BF16)
        d = _dot(vt_ref[:, n * BLK:(n + 1) * BLK], p)
        acc = d if acc is None else acc + d
        yield
    acc_ref[item_id] = acc


def _rounds(nb):
    tail = min(3, nb)
    return [(j,) for j in range(nb - 1, tail - 1, -1)] + [tuple(range(tail - 1, -1, -1))]


def _round_tiles(nb):
    return max(sum(2 * (j + 1) for j in js) for js in _rounds(nb))


def _pipelined(nb, first, second, last, prepare=None):
    one = jnp.minimum(pl.program_id(0) + 1, 1)
    rounds = []
    for js in _rounds(nb):
        items, slot = [], 0
        for j in js:
            for half in range(2):
                items.append(((j, half), slot))
                slot += j + 1
        rounds.append(items)
    if prepare is not None:
        for item, _slot in rounds[0]:
            prepare(item)
    for r in range(len(rounds) + 2):
        def round_body(_, carry, r=r):
            if 0 <= r - 2 < len(rounds):
                for item, _slot in rounds[r - 2]:
                    last(item)
            if prepare is not None and r + 1 < len(rounds):
                for item, _slot in rounds[r + 1]:
                    prepare(item)
            gens = []
            if r < len(rounds):
                gens += [first(item, r % 2, slot0) for item, slot0 in rounds[r]]
            if 0 <= r - 1 < len(rounds):
                gens += [second(item, (r - 1) % 2, slot0) for item, slot0 in rounds[r - 1]]
            while gens:
                for g in list(gens):
                    if next(g, StopIteration) is StopIteration:
                        gens.remove(g)
            return carry

        lax.fori_loop(0, one, round_body, 0)


def _attn_scratch(S, vt_rows):
    nb = S // BLK
    return [
        pltpu.VMEM((2, _round_tiles(nb), BLK, BLK), F32),
        pltpu.VMEM((2 * nb, 8, BLK), F32),
        pltpu.VMEM((2 * nb, vt_rows, BLK), F32),
    ]


def _tile_spec(shape, off):
    return pl.BlockSpec((1, 1) + shape, lambda b, h: (b, off + h) + (0,) * len(shape))


def _moba_kernel(qa_in, ka_in, vt_in, ksum_in, causal_ref, o_ref, s_ref, m_ref, acc_ref, qa_ref,
                 g_ref):
    S = ka_in.shape[3]
    nb = S // BLK
    lane = lax.broadcasted_iota(jnp.int32, (1, LANES), 1)
    row8 = lax.broadcasted_iota(jnp.int32, (nb, BLK), 0)

    kmean = jnp.concatenate([ksum_in[0, i, 0] for i in range(ksum_in.shape[1])], axis=0)
    kmean = kmean * (1.0 / BLK)
    for hh in range(2):
        in_head = (lane >= hh * HEAD_DIM) & (lane < (hh + 1) * HEAD_DIM)
        km = jnp.where(in_head, kmean, 0.0)
        km_hi = km.astype(BF16)
        km_lo = (km - km_hi.astype(F32)).astype(BF16)
        g2 = _dot(jnp.concatenate([km_hi, km_lo], axis=0), qa_in[0, 0, hh])
        g_ref[hh] = g2[:nb] + g2[nb:]

    n_sel = max(1, min(MOBA_TOPK, nb - 1))

    def prepare(item):
        j, hh = item
        cols = slice(j * BLK, (j + 1) * BLK)
        qa_ref[2 * j + hh] = qa_in[0, 0, hh, :, cols]
        if j > 0:
            gj = g_ref[hh, :, cols]
            cnt = jnp.zeros((nb, BLK), F32)
            for m_ in range(j):
                gm = gj[m_:m_ + 1, :]
                beats = (gm > gj) | ((gm == gj) & (m_ < row8))
                cnt = cnt + beats.astype(F32)
            pen = jnp.where((row8 >= j) | (cnt < n_sel), 0.0, NEG)
            off = HEAD_DIM * (1 - hh) + EXTRA_PEN
            pen16 = jnp.concatenate([pen, jnp.zeros((16 - nb, BLK), F32)], axis=0)
            qa_ref[2 * j + hh, off:off + 16, :] = pen16.astype(BF16)

    def scores(item, par, slot0):
        j, hh = item
        return _scores_t(qa_ref.at[2 * j + hh], ka_in.at[0, 0, hh], causal_ref, s_ref, m_ref,
                         2 * j + hh, par, slot0, j)

    def values(item, par, slot0):
        j, hh = item
        return _weighted_values_t(s_ref, m_ref, acc_ref, 2 * j + hh, par, slot0,
                                  vt_in.at[0, 0, hh], j)

    def finish(item):
        j, hh = item
        if hh == 1:
            halves = []
            for h2 in range(2):
                acc = acc_ref[2 * j + h2]
                halves.append(acc[:HEAD_DIM] / acc[HEAD_DIM:HEAD_DIM + 1])
            o_t = jnp.concatenate(halves, axis=0)
            o_ref[0, j * BLK:(j + 1) * BLK, :] = o_t.T.astype(BF16)

    _pipelined(nb, scores, values, finish, prepare)


def _moba(ka, qa, vt_m, ksum):
    B, _, _, S, _ = ka.shape
    nb = S // BLK
    assert nb <= 8
    rows = HEAD_DIM + ONES_ROWS
    return pl.pallas_call(
        _moba_kernel,
        grid=(B, N_PAIRS),
        in_specs=[
            _tile_spec((2, LANES, S), 0), _tile_spec((2, S, LANES), 0),
            _tile_spec((2, rows, S), 0),
            pl.BlockSpec((1, ksum.shape[1], 1) + ksum.shape[3:], lambda b, h: (b, 0, h, 0, 0)),
            _CAUSAL_SPEC,
        ],
        out_specs=pl.BlockSpec((1, S, LANES), lambda b, h: (b, 0, h)),
        out_shape=jax.ShapeDtypeStruct((B, S, MOBA_WIDTH), BF16),
        scratch_shapes=_attn_scratch(S, rows) + [
            pltpu.VMEM((2 * nb, LANES, BLK), BF16),
            pltpu.VMEM((2, nb, S), F32),
        ],
        compiler_params=pltpu.CompilerParams(
            dimension_semantics=("arbitrary", "arbitrary"), vmem_limit_bytes=VMEM_LIMIT),
        name="moba",
    )(qa, ka, vt_m, ksum, jnp.asarray(_causal_tile()))


def _diff_kernel(qa_in, ka_in, vt_in, causal_ref, lq1_ref, lk1_ref, lq2_ref, lk2_ref, g_ref,
                 o_ref, s_ref, m_ref, acc_ref, *, lam_init):
    S = ka_in.shape[3]
    nb = S // BLK
    dv = 2 * DIFF_HEAD_DIM

    def scores(item, par, slot0):
        j, mp = item
        return _scores_t(qa_in.at[0, 0, mp, :, pl.ds(j * BLK, BLK)], ka_in.at[0, 0, mp],
                         causal_ref, s_ref, m_ref, 2 * j + mp, par, slot0, j)

    def values(item, par, slot0):
        j, mp = item
        return _weighted_values_t(s_ref, m_ref, acc_ref, 2 * j + mp, par, slot0,
                                  vt_in.at[0, 0], j)

    def finish(item):
        j, mp = item
        if mp == 1:
            lam = (jnp.exp(jnp.sum(lq1_ref[...] * lk1_ref[...], axis=-1, keepdims=True))
                   - jnp.exp(jnp.sum(lq2_ref[...] * lk2_ref[...], axis=-1, keepdims=True))
                   + lam_init)
            outs = []
            for m2 in range(2):
                acc = acc_ref[2 * j + m2]
                outs.append(acc[:dv] / acc[dv:dv + 1])
            a = (outs[0] - lam * outs[1]).T
            y = _rms(a, g_ref[...], SUBLN_EPS) * (1.0 - lam_init)
            o_ref[0, j * BLK:(j + 1) * BLK, :] = y.astype(BF16)

    _pipelined(nb, scores, values, finish)


def _diff(ka, qa, vt_d, lq1, lk1, lq2, lk2, subln_g, lam_init):
    B, _, _, S, _ = ka.shape
    H = DIFF_HEADS
    vec = lambda n: pl.BlockSpec((1, n), lambda b, h: (0, 0))
    d = DIFF_HEAD_DIM
    rows = 2 * d + ONES_ROWS
    return pl.pallas_call(
        functools.partial(_diff_kernel, lam_init=lam_init),
        grid=(B, H),
        in_specs=[
            _tile_spec((2, LANES, S), N_PAIRS), _tile_spec((2, S, LANES), N_PAIRS),
            _tile_spec((rows, S), 0), _CAUSAL_SPEC,
            vec(d), vec(d), vec(d), vec(d), vec(2 * d),
        ],
        out_specs=pl.BlockSpec((1, S, LANES), lambda b, h: (b, 0, h)),
        out_shape=jax.ShapeDtypeStruct((B, S, DIFF_WIDTH), BF16),
        scratch_shapes=_attn_scratch(S, rows),
        compiler_params=pltpu.CompilerParams(
            dimension_semantics=("arbitrary", "arbitrary"), vmem_limit_bytes=VMEM_LIMIT),
        name="diff_attn",
    )(qa, ka, vt_d, jnp.asarray(_causal_tile()), lq1.reshape(1, d), lk1.reshape(1, d),
      lq2.reshape(1, d), lk2.reshape(1, d), subln_g.reshape(1, 2 * d))


def _merge_kernel(x_ref, ya_ref, yb_ref, gpre_ref, wga_ref, wgb_ref, wa_ref, wb_ref, wo_ref,
                  gpost_ref, o_ref, *, parts):
    tm, D = x_ref.shape
    rows = [slice(p * tm // parts, (p + 1) * tm // parts) for p in range(parts)]
    hs = [_rms(x_ref[r, :], gpre_ref[0], NORM_EPS).astype(BF16) for r in rows]
    for r, h in zip(rows, hs):
        ga = _dot(h, wga_ref[0])
        a = _dot(ya_ref[r, :], wa_ref[0])
        merged = jax.nn.sigmoid(ga) * a
        gb = _dot(h, wgb_ref[0])
        b = _dot(yb_ref[r, :], wb_ref[0])
        merged = merged + jax.nn.sigmoid(gb) * b
        o = _dot(merged.astype(BF16), wo_ref[0])
        o_ref[r, :] = x_ref[r, :] + _rms(o, gpost_ref[0], NORM_EPS)


def _merge(x2, ya2, yb2, gpre, w_in, wa, wb, wo, gpost, l, tm=1024):
    T, D = x2.shape
    gate_col = QKV_COLS // D
    return pl.pallas_call(
        functools.partial(_merge_kernel, parts=4),
        grid=(T // tm,),
        in_specs=[
            pl.BlockSpec((tm, D), lambda i: (i, 0)),
            pl.BlockSpec((tm, MOBA_WIDTH), lambda i: (i, 0)),
            pl.BlockSpec((tm, DIFF_WIDTH), lambda i: (i, 0)),
            _layer_spec((1, D), l),
            _layer_spec((D, D), l, gate_col),
            _layer_spec((D, D), l, gate_col + 1),
            _layer_spec((MOBA_WIDTH, D), l),
            _layer_spec((DIFF_WIDTH, D), l),
            _layer_spec((D, D), l),
            _layer_spec((1, D), l),
        ],
        out_specs=pl.BlockSpec((tm, D), lambda i: (i, 0)),
        out_shape=jax.ShapeDtypeStruct((T, D), F32),
        compiler_params=pltpu.CompilerParams(
            dimension_semantics=("arbitrary",), vmem_limit_bytes=VMEM_LIMIT),
        name="merge",
    )(x2, ya2, yb2, gpre[:, None], w_in, w_in, wa, wb, wo, gpost[:, None])


def _ffn_kernel(x_ref, gpre_ref, wg_ref, wu_ref, wd_ref, gpost_ref, o_ref, a_ref, *, chunk, parts):
    tm = x_ref.shape[0]
    rows = [slice(p * tm // parts, (p + 1) * tm // parts) for p in range(parts)]
    hs = [_rms(x_ref[r, :], gpre_ref[0], NORM_EPS).astype(BF16) for r in rows]
    for c in range(D_FF // chunk):
        cols = slice(c * chunk, (c + 1) * chunk)
        for r, h in zip(rows, hs):
            g = _dot(h, wg_ref[0, :, cols])
            u = _dot(h, wu_ref[0, :, cols])
            a_ref[r, cols] = (jax.nn.silu(g) * u).astype(BF16)
    for r in rows:
        f = _dot(a_ref[r, :], wd_ref[0])
        o_ref[r, :] = x_ref[r, :] + _rms(f, gpost_ref[0], NORM_EPS)


def _ffn(x2, gpre, wg, wu, wd, gpost, l, tm=1024):
    T, D = x2.shape
    return pl.pallas_call(
        functools.partial(_ffn_kernel, chunk=D_FF // 2, parts=4),
        grid=(T // tm,),
        in_specs=[
            pl.BlockSpec((tm, D), lambda i: (i, 0)),
            _layer_spec((1, D), l),
            _layer_spec((D, D_FF), l),
            _layer_spec((D, D_FF), l),
            _layer_spec((D_FF, D), l),
            _layer_spec((1, D), l),
        ],
        out_specs=pl.BlockSpec((tm, D), lambda i: (i, 0)),
        out_shape=jax.ShapeDtypeStruct((T, D), F32),
        scratch_shapes=[pltpu.VMEM((tm, D_FF), BF16)],
        compiler_params=pltpu.CompilerParams(
            dimension_semantics=("arbitrary",), vmem_limit_bytes=VMEM_LIMIT),
        name="ffn",
    )(x2, gpre[:, None], wg, wu, wd, gpost[:, None])


def kernel(x, norm_mix_pre_g, w_in, w_branch_a, w_branch_b, lam_q1, lam_k1, lam_q2, lam_k2,
           diff_subln_g, w_out, norm_mix_post_g, norm_ffn_pre_g, w_gate, w_up, w_down,
           norm_ffn_post_g):
    B, S, D = x.shape
    assert D == D_MODEL and S % BLK == 0
    depth = w_in.shape[0]
    for l in range(depth):
        lam_init = 0.8 - 0.6 * math.exp(-0.3 * l)
        ka, qa, vt_m, vt_d, ksum = _in_proj(x, norm_mix_pre_g, w_in, l)
        ya = _moba(ka, qa, vt_m, ksum)
        yb = _diff(ka, qa, vt_d, lam_q1[l], lam_k1[l], lam_q2[l], lam_k2[l],
                   diff_subln_g[l], lam_init)
        x2 = _merge(x.reshape(B * S, D), ya.reshape(B * S, MOBA_WIDTH),
                    yb.reshape(B * S, DIFF_WIDTH), norm_mix_pre_g, w_in, w_branch_a,
                    w_branch_b, w_out, norm_mix_post_g, l)
        x2 = _ffn(x2, norm_ffn_pre_g, w_gate.astype(BF16), w_up.astype(BF16),
                  w_down.astype(BF16), norm_ffn_post_g, l)
        x = x2.reshape(B, S, D)
    return x
```

```python
import functools
import math

import numpy as np
import jax
import jax.numpy as jnp
from jax import lax
from jax.experimental import pallas as pl
from jax.experimental.pallas import tpu as pltpu

D_MODEL = 1024
MOBA_HEADS = 8
MOBA_TOPK = 3
MOBA_WIDTH = 512
DIFF_HEADS = 4
DIFF_HEAD_DIM = 64
DIFF_WIDTH = 512
HEAD_DIM = 64
QKV_COLS = 3 * MOBA_WIDTH + 3 * DIFF_WIDTH
D_FF = 2816
NORM_EPS = 1e-6
SUBLN_EPS = 1e-5
ALIBI_MAX_BIAS = 8.0
NEG = -1e30
LOG2E = math.log2(math.e)
Q_SCALE = HEAD_DIM ** -0.5 * LOG2E

LANES = 128
BLK = 256
N_COL_TILES = QKV_COLS // LANES
N_QK_TILES = N_COL_TILES // 3
N_PAIRS = MOBA_HEADS // 2
ONES_ROWS = 16
EXTRA_PEN = 16
VMEM_LIMIT = 56 * 1024 * 1024

BF16 = jnp.bfloat16
F32 = jnp.float32


def _alibi_slopes():
    n = MOBA_HEADS + DIFF_HEADS
    slopes = 2.0 ** (-ALIBI_MAX_BIAS * np.arange(1, n + 1) / n)
    stride = n // DIFF_HEADS
    diff_idx = np.arange(DIFF_HEADS) * stride + (stride - 1)
    moba_idx = np.setdiff1d(np.arange(n), diff_idx)
    return (np.asarray(slopes[moba_idx], np.float32), np.asarray(slopes[diff_idx], np.float32))


def _split3(v):
    parts, rem = [], float(v)
    for _ in range(3):
        p = float(np.asarray(rem, np.float32).astype(BF16).astype(np.float64))
        parts.append(p)
        rem -= p
    return parts


def _attn_tables(slopes, S):
    nb = S // BLK
    H = len(slopes)
    qx = np.zeros((H, 2, nb, LANES), np.float32)
    kx = np.zeros((2, S, LANES), np.float32)
    pos = np.arange(S)
    for half in range(2):
        e0 = HEAD_DIM * (1 - half)
        for c in range(3):
            kx[half, :, e0 + c] = pos % BLK
            kx[half, :, e0 + 3 + c] = (pos // BLK) * BLK
            kx[half, :, e0 + 6 + c] = 1.0
        for n in range(nb):
            kx[half, n * BLK:(n + 1) * BLK, e0 + EXTRA_PEN + n] = 1.0
        for h in range(H):
            s = float(np.float32(slopes[h])) * LOG2E
            sp = _split3(s)
            for j in range(nb):
                cp = _split3(-s * j * BLK)
                for c in range(3):
                    qx[h, half, j, e0 + c] = sp[c]
                    qx[h, half, j, e0 + 3 + c] = sp[c]
                    qx[h, half, j, e0 + 6 + c] = cp[c]
    return qx.transpose(0, 1, 3, 2), kx


def _query_extras(S, per_step):
    nb = S // BLK
    moba_slopes, diff_slopes = _alibi_slopes()
    qx_m, _ = _attn_tables(moba_slopes, S)
    qx_d, _ = _attn_tables(diff_slopes, S)
    qx_m = qx_m.reshape(N_PAIRS, 2, 2, LANES, nb)[:, np.arange(2), np.arange(2)]
    qx = np.concatenate([qx_m, qx_d], axis=0)
    qx = qx.reshape(N_QK_TILES, 2, LANES, nb // per_step, per_step)
    return np.ascontiguousarray(qx.transpose(3, 0, 1, 2, 4))


def _dot(a, b):
    return jnp.dot(a, b, preferred_element_type=F32)


def _dot_tn(a_t, b):
    return lax.dot_general(a_t, b, (((0,), (0,)), ((), ())), preferred_element_type=F32)


def _rms(x, g, eps):
    return x * lax.rsqrt(jnp.mean(x * x, axis=-1, keepdims=True) + eps) * g


def _layer_spec(block, l, col_block=0):
    index = (l,) + (0,) * (len(block) - 1) + (col_block,)
    return pl.BlockSpec((1,) + tuple(block), lambda *_: index, pipeline_mode=pl.Buffered(1))


def _in_proj_kernel(x_ref, g_ref, w_ref, kx_ref, qx_ref, ka_out, qa_out, vtm_out, vtd_out,
                    ksum_out, *, parts):
    tm = x_ref.shape[1]
    part = tm // parts
    assert part == BLK
    chunk = 4 * LANES
    lane = lax.broadcasted_iota(jnp.int32, (1, LANES), 1)
    row = lax.broadcasted_iota(jnp.int32, (LANES, 1), 0)
    in_half = [(lane >= h * HEAD_DIM) & (lane < (h + 1) * HEAD_DIM) for h in range(2)]
    half_rows = [(row >= h * HEAD_DIM) & (row < (h + 1) * HEAD_DIM) for h in range(2)]
    ones = jnp.ones((ONES_ROWS, part), BF16)
    rows = [slice(p * part, (p + 1) * part) for p in range(parts)]
    hs = [_rms(x_ref[0, r, :], g_ref[0], NORM_EPS).astype(BF16) for r in rows]
    for p, (r, h) in enumerate(zip(rows, hs)):
        for c in range(QKV_COLS // chunk):
            acc = _dot(h, w_ref[0, :, c * chunk:(c + 1) * chunk])
            kind, mixer = ("q", "k", "v")[c % 3], c // 3
            for t in range(4):
                tile = 4 * mixer + t
                val = acc[:, t * LANES:(t + 1) * LANES]
                if kind == "k":
                    if mixer == 0:
                        ksum_out[0, 0, t, p:p + 1, :] = jnp.sum(val, axis=0, keepdims=True)
                    val = val.astype(BF16)
                    for half in range(2):
                        ka_out[0, tile, half, r, :] = jnp.where(
                            in_half[half], val, kx_ref[half, r, :])
                elif kind == "q":
                    val_t = (val * Q_SCALE).astype(BF16).T
                    for half in range(2):
                        ext = jnp.broadcast_to(qx_ref[0, tile, half][:, p:p + 1], (LANES, part))
                        qa_out[0, tile, half, :, r] = jnp.where(
                            half_rows[half], val_t, ext.astype(BF16))
                else:
                    val_t = val.astype(BF16).T
                    if mixer == 0:
                        for hh in range(2):
                            vtm_out[0, t, hh, :, r] = jnp.concatenate(
                                [val_t[hh * HEAD_DIM:(hh + 1) * HEAD_DIM], ones], axis=0)
                    else:
                        vtd_out[0, t, :, r] = jnp.concatenate([val_t, ones], axis=0)


def _in_proj(x, g, w_in, l, tm=1024):
    B, S, D = x.shape
    parts = tm // BLK
    kx = jnp.asarray(_attn_tables(_alibi_slopes()[0], S)[1], BF16)
    qx = jnp.asarray(_query_extras(S, parts))
    n, dv = N_QK_TILES, 2 * DIFF_HEAD_DIM
    return pl.pallas_call(
        functools.partial(_in_proj_kernel, parts=parts),
        grid=(B, S // tm),
        in_specs=[
            pl.BlockSpec((1, tm, D), lambda b, i: (b, i, 0)),
            _layer_spec((1, D), l),
            _layer_spec((D, QKV_COLS), l),
            pl.BlockSpec((2, tm, LANES), lambda b, i: (0, i, 0)),
            pl.BlockSpec((1, n, 2, LANES, parts), lambda b, i: (i, 0, 0, 0, 0)),
        ],
        out_specs=[
            pl.BlockSpec((1, n, 2, tm, LANES), lambda b, i: (b, 0, 0, i, 0)),
            pl.BlockSpec((1, n, 2, LANES, tm), lambda b, i: (b, 0, 0, 0, i)),
            pl.BlockSpec((1, N_PAIRS, 2, HEAD_DIM + ONES_ROWS, tm), lambda b, i: (b, 0, 0, 0, i)),
            pl.BlockSpec((1, DIFF_HEADS, dv + ONES_ROWS, tm), lambda b, i: (b, 0, 0, i)),
            pl.BlockSpec((1, 1, N_PAIRS, parts, LANES), lambda b, i: (b, i, 0, 0, 0)),
        ],
        out_shape=[
            jax.ShapeDtypeStruct((B, n, 2, S, LANES), BF16),
            jax.ShapeDtypeStruct((B, n, 2, LANES, S), BF16),
            jax.ShapeDtypeStruct((B, N_PAIRS, 2, HEAD_DIM + ONES_ROWS, S), BF16),
            jax.ShapeDtypeStruct((B, DIFF_HEADS, dv + ONES_ROWS, S), BF16),
            jax.ShapeDtypeStruct((B, S // tm, N_PAIRS, parts, LANES), F32),
        ],
        compiler_params=pltpu.CompilerParams(
            dimension_semantics=("arbitrary", "arbitrary"), vmem_limit_bytes=VMEM_LIMIT),
        name="in_proj",
    )(x, g[:, None], w_in, kx, qx)


def _causal_tile():
    r = np.arange(BLK)
    return np.where(r[:, None] <= r[None, :], 0.0, NEG).astype(np.float32)


_CAUSAL_SPEC = pl.BlockSpec((BLK, BLK), lambda b, h: (0, 0))


def _scores_t(qa_tile, ka_ref, causal_ref, s_ref, m_ref, item_id, par, slot0, j):
    m8 = None
    for t, n in enumerate(range(j, -1, -1)):
        s = _dot(ka_ref[n * BLK:(n + 1) * BLK, :], qa_tile[...])
        if n == j:
            s = s + causal_ref[...]
        s_ref[par, slot0 + t] = s
        t8 = jnp.max(s.reshape(BLK // 8, 8, BLK), axis=0)
        m8 = t8 if m8 is None else jnp.maximum(m8, t8)
        yield
    m_ref[item_id] = jnp.broadcast_to(jnp.max(m8, axis=0, keepdims=True), (8, BLK))


def _weighted_values_t(s_ref, m_ref, acc_ref, item_id, par, slot0, vt_ref, j):
    acc = None
    m = m_ref[item_id][:1]
    for t, n in enumerate(range(j, -1, -1)):
        p = jnp.exp2(s_ref[par, slot0 + t] - m).astype(BF16)
        d = _dot(vt_ref[:, n * BLK:(n + 1) * BLK], p)
        acc = d if acc is None else acc + d
        yield
    acc_ref[item_id] = acc


def _rounds(nb):
    tail = min(3, nb)
    return [(j,) for j in range(nb - 1, tail - 1, -1)] + [tuple(range(tail - 1, -1, -1))]


def _round_tiles(nb):
    return max(sum(2 * (j + 1) for j in js) for js in _rounds(nb))


def _pipelined(nb, first, second, last, prepare=None):
    one = jnp.minimum(pl.program_id(0) + 1, 1)
    rounds = []
    for js in _rounds(nb):
        items, slot = [], 0
        for j in js:
            for half in range(2):
                items.append(((j, half), slot))
                slot += j + 1
        rounds.append(items)
    if prepare is not None:
        for item, _slot in rounds[0]:
            prepare(item)
    for r in range(len(rounds) + 2):
        def round_body(_, carry, r=r):
            if 0 <= r - 2 < len(rounds):
                for item, _slot in rounds[r - 2]:
                    last(item)
            if prepare is not None and r + 1 < len(rounds):
                for item, _slot in rounds[r + 1]:
                    prepare(item)
            gens = []
            if r < len(rounds):
                gens += [first(item, r % 2, slot0) for item, slot0 in rounds[r]]
            if 0 <= r - 1 < len(rounds):
                gens += [second(item, (r - 1) % 2, slot0) for item, slot0 in rounds[r - 1]]
            while gens:
                for g in list(gens):
                    if next(g, StopIteration) is StopIteration:
                        gens.remove(g)
            return carry

        lax.fori_loop(0, one, round_body, 0)


def _attn_scratch(S, vt_rows):
    nb = S // BLK
    return [
        pltpu.VMEM((2, _round_tiles(nb), BLK, BLK), F32),
        pltpu.VMEM((2 * nb, 8, BLK), F32),
        pltpu.VMEM((2 * nb, vt_rows, BLK), F32),
    ]


def _tile_spec(shape, off):
    return pl.BlockSpec((1, 1) + shape, lambda b, h: (b, off + h) + (0,) * len(shape))


def _moba_kernel(qa_in, ka_in, vt_in, ksum_in, causal_ref, o_ref, s_ref, m_ref, acc_ref, qa_ref,
                 g_ref):
    S = ka_in.shape[3]
    nb = S // BLK
    lane = lax.broadcasted_iota(jnp.int32, (1, LANES), 1)
    row8 = lax.broadcasted_iota(jnp.int32, (nb, BLK), 0)

    kmean = jnp.concatenate([ksum_in[0, i, 0] for i in range(ksum_in.shape[1])], axis=0)
    kmean = kmean * (1.0 / BLK)
    for hh in range(2):
        in_head = (lane >= hh * HEAD_DIM) & (lane < (hh + 1) * HEAD_DIM)
        km = jnp.where(in_head, kmean, 0.0)
        km_hi = km.astype(BF16)
        km_lo = (km - km_hi.astype(F32)).astype(BF16)
        g2 = _dot(jnp.concatenate([km_hi, km_lo], axis=0), qa_in[0, 0, hh])
        g_ref[hh] = g2[:nb] + g2[nb:]

    n_sel = max(1, min(MOBA_TOPK, nb - 1))

    def prepare(item):
        j, hh = item
        cols = slice(j * BLK, (j + 1) * BLK)
        qa_ref[2 * j + hh] = qa_in[0, 0, hh, :, cols]
        if j > 0:
            gj = g_ref[hh, :, cols]
            cnt = jnp.zeros((nb, BLK), F32)
            for m_ in range(j):
                gm = gj[m_:m_ + 1, :]
                beats = (gm > gj) | ((gm == gj) & (m_ < row8))
                cnt = cnt + beats.astype(F32)
            pen = jnp.where((row8 >= j) | (cnt < n_sel), 0.0, NEG)
            off = HEAD_DIM * (1 - hh) + EXTRA_PEN
            pen16 = jnp.concatenate([pen, jnp.zeros((16 - nb, BLK), F32)], axis=0)
            qa_ref[2 * j + hh, off:off + 16, :] = pen16.astype(BF16)

    def scores(item, par, slot0):
        j, hh = item
        return _scores_t(qa_ref.at[2 * j + hh], ka_in.at[0, 0, hh], causal_ref, s_ref, m_ref,
                         2 * j + hh, par, slot0, j)

    def values(item, par, slot0):
        j, hh = item
        return _weighted_values_t(s_ref, m_ref, acc_ref, 2 * j + hh, par, slot0,
                                  vt_in.at[0, 0, hh], j)

    def finish(item):
        j, hh = item
        if hh == 1:
            halves = []
            for h2 in range(2):
                acc = acc_ref[2 * j + h2]
                halves.append(acc[:HEAD_DIM] / acc[HEAD_DIM:HEAD_DIM + 1])
            o_t = jnp.concatenate(halves, axis=0)
            o_ref[0, 0, :, j * BLK:(j + 1) * BLK] = o_t.astype(BF16)

    _pipelined(nb, scores, values, finish, prepare)


def _moba(ka, qa, vt_m, ksum):
    B, _, _, S, _ = ka.shape
    nb = S // BLK
    assert nb <= 8
    rows = HEAD_DIM + ONES_ROWS
    return pl.pallas_call(
        _moba_kernel,
        grid=(B, N_PAIRS),
        in_specs=[
            _tile_spec((2, LANES, S), 0), _tile_spec((2, S, LANES), 0),
            _tile_spec((2, rows, S), 0),
            pl.BlockSpec((1, ksum.shape[1], 1) + ksum.shape[3:], lambda b, h: (b, 0, h, 0, 0)),
            _CAUSAL_SPEC,
        ],
        out_specs=pl.BlockSpec((1, 1, LANES, S), lambda b, h: (b, h, 0, 0)),
        out_shape=jax.ShapeDtypeStruct((B, N_PAIRS, LANES, S), BF16),
        scratch_shapes=_attn_scratch(S, rows) + [
            pltpu.VMEM((2 * nb, LANES, BLK), BF16),
            pltpu.VMEM((2, nb, S), F32),
        ],
        compiler_params=pltpu.CompilerParams(
            dimension_semantics=("arbitrary", "arbitrary"), vmem_limit_bytes=VMEM_LIMIT),
        name="moba",
    )(qa, ka, vt_m, ksum, jnp.asarray(_causal_tile()))


def _diff_kernel(qa_in, ka_in, vt_in, causal_ref, lq1_ref, lk1_ref, lq2_ref, lk2_ref, g_ref,
                 o_ref, s_ref, m_ref, acc_ref, *, lam_init):
    S = ka_in.shape[3]
    nb = S // BLK
    dv = 2 * DIFF_HEAD_DIM

    def scores(item, par, slot0):
        j, mp = item
        return _scores_t(qa_in.at[0, 0, mp, :, pl.ds(j * BLK, BLK)], ka_in.at[0, 0, mp],
                         causal_ref, s_ref, m_ref, 2 * j + mp, par, slot0, j)

    def values(item, par, slot0):
        j, mp = item
        return _weighted_values_t(s_ref, m_ref, acc_ref, 2 * j + mp, par, slot0,
                                  vt_in.at[0, 0], j)

    def finish(item):
        j, mp = item
        if mp == 1:
            lam = (jnp.exp(jnp.sum(lq1_ref[...] * lk1_ref[...], axis=-1, keepdims=True))
                   - jnp.exp(jnp.sum(lq2_ref[...] * lk2_ref[...], axis=-1, keepdims=True))
                   + lam_init)
            outs = []
            for m2 in range(2):
                acc = acc_ref[2 * j + m2]
                outs.append(acc[:dv] / acc[dv:dv + 1])
            a = outs[0] - lam * outs[1]
            inv = lax.rsqrt(jnp.mean(a * a, axis=0, keepdims=True) + SUBLN_EPS)
            y = a * inv * g_ref[...] * (1.0 - lam_init)
            o_ref[0, 0, :, j * BLK:(j + 1) * BLK] = y.astype(BF16)

    _pipelined(nb, scores, values, finish)


def _diff(ka, qa, vt_d, lq1, lk1, lq2, lk2, subln_g, lam_init):
    B, _, _, S, _ = ka.shape
    H = DIFF_HEADS
    vec = lambda n: pl.BlockSpec((1, n), lambda b, h: (0, 0))
    d = DIFF_HEAD_DIM
    rows = 2 * d + ONES_ROWS
    return pl.pallas_call(
        functools.partial(_diff_kernel, lam_init=lam_init),
        grid=(B, H),
        in_specs=[
            _tile_spec((2, LANES, S), N_PAIRS), _tile_spec((2, S, LANES), N_PAIRS),
            _tile_spec((rows, S), 0), _CAUSAL_SPEC,
            vec(d), vec(d), vec(d), vec(d),
            pl.BlockSpec((2 * d, 1), lambda b, h: (0, 0)),
        ],
        out_specs=pl.BlockSpec((1, 1, LANES, S), lambda b, h: (b, h, 0, 0)),
        out_shape=jax.ShapeDtypeStruct((B, H, LANES, S), BF16),
        scratch_shapes=_attn_scratch(S, rows),
        compiler_params=pltpu.CompilerParams(
            dimension_semantics=("arbitrary", "arbitrary"), vmem_limit_bytes=VMEM_LIMIT),
        name="diff_attn",
    )(qa, ka, vt_d, jnp.asarray(_causal_tile()), lq1.reshape(1, d), lk1.reshape(1, d),
      lq2.reshape(1, d), lk2.reshape(1, d), subln_g.reshape(2 * d, 1))


def _merge_kernel(x_ref, ya_ref, yb_ref, gpre_ref, wga_ref, wgb_ref, wa_ref, wb_ref, wo_ref,
                  gpost_ref, o_ref, *, parts):
    _, tm, D = x_ref.shape
    rows = [slice(p * tm // parts, (p + 1) * tm // parts) for p in range(parts)]
    hs = [_rms(x_ref[0, r, :], gpre_ref[0], NORM_EPS).astype(BF16) for r in rows]
    for r, h in zip(rows, hs):
        ga = _dot(h, wga_ref[0])
        a = _dot_tn(ya_ref[0, :, r], wa_ref[0])
        merged = jax.nn.sigmoid(ga) * a
        gb = _dot(h, wgb_ref[0])
        b = _dot_tn(yb_ref[0, :, r], wb_ref[0])
        merged = merged + jax.nn.sigmoid(gb) * b
        o = _dot(merged.astype(BF16), wo_ref[0])
        o_ref[0, r, :] = x_ref[0, r, :] + _rms(o, gpost_ref[0], NORM_EPS)


def _merge(x, ya_t, yb_t, gpre, w_in, wa, wb, wo, gpost, l, tm=1024):
    B, S, D = x.shape
    gate_col = QKV_COLS // D
    return pl.pallas_call(
        functools.partial(_merge_kernel, parts=4),
        grid=(B, S // tm),
        in_specs=[
            pl.BlockSpec((1, tm, D), lambda b, i: (b, i, 0)),
            pl.BlockSpec((1, MOBA_WIDTH, tm), lambda b, i: (b, 0, i)),
            pl.BlockSpec((1, DIFF_WIDTH, tm), lambda b, i: (b, 0, i)),
            _layer_spec((1, D), l),
            _layer_spec((D, D), l, gate_col),
            _layer_spec((D, D), l, gate_col + 1),
            _layer_spec((MOBA_WIDTH, D), l),
            _layer_spec((DIFF_WIDTH, D), l),
            _layer_spec((D, D), l),
            _layer_spec((1, D), l),
        ],
        out_specs=pl.BlockSpec((1, tm, D), lambda b, i: (b, i, 0)),
        out_shape=jax.ShapeDtypeStruct((B, S, D), F32),
        compiler_params=pltpu.CompilerParams(
            dimension_semantics=("arbitrary", "arbitrary"), vmem_limit_bytes=VMEM_LIMIT),
        name="merge",
    )(x, ya_t, yb_t, gpre[:, None], w_in, w_in, wa, wb, wo, gpost[:, None])


def _ffn_kernel(x_ref, gpre_ref, wg_ref, wu_ref, wd_ref, gpost_ref, o_ref, a_ref, *, chunk, parts):
    tm = x_ref.shape[0]
    rows = [slice(p * tm // parts, (p + 1) * tm // parts) for p in range(parts)]
    hs = [_rms(x_ref[r, :], gpre_ref[0], NORM_EPS).astype(BF16) for r in rows]
    for c in range(D_FF // chunk):
        cols = slice(c * chunk, (c + 1) * chunk)
        for r, h in zip(rows, hs):
            g = _dot(h, wg_ref[0, :, cols])
            u = _dot(h, wu_ref[0, :, cols])
            a_ref[r, cols] = (jax.nn.silu(g) * u).astype(BF16)
    for r in rows:
        f = _dot(a_ref[r, :], wd_ref[0])
        o_ref[r, :] = x_ref[r, :] + _rms(f, gpost_ref[0], NORM_EPS)


def _ffn(x2, gpre, wg, wu, wd, gpost, l, tm=1024):
    T, D = x2.shape
    return pl.pallas_call(
        functools.partial(_ffn_kernel, chunk=D_FF // 2, parts=4),
        grid=(T // tm,),
        in_specs=[
            pl.BlockSpec((tm, D), lambda i: (i, 0)),
            _layer_spec((1, D), l),
            _layer_spec((D, D_FF), l),
            _layer_spec((D, D_FF), l),
            _layer_spec((D_FF, D), l),
            _layer_spec((1, D), l),
        ],
        out_specs=pl.BlockSpec((tm, D), lambda i: (i, 0)),
        out_shape=jax.ShapeDtypeStruct((T, D), F32),
        scratch_shapes=[pltpu.VMEM((tm, D_FF), BF16)],
        compiler_params=pltpu.CompilerParams(
            dimension_semantics=("arbitrary",), vmem_limit_bytes=VMEM_LIMIT),
        name="ffn",
    )(x2, gpre[:, None], wg, wu, wd, gpost[:, None])


def kernel(x, norm_mix_pre_g, w_in, w_branch_a, w_branch_b, lam_q1, lam_k1, lam_q2, lam_k2,
           diff_subln_g, w_out, norm_mix_post_g, norm_ffn_pre_g, w_gate, w_up, w_down,
           norm_ffn_post_g):
    B, S, D = x.shape
    assert D == D_MODEL and S % BLK == 0
    depth = w_in.shape[0]
    for l in range(depth):
        lam_init = 0.8 - 0.6 * math.exp(-0.3 * l)
        ka, qa, vt_m, vt_d, ksum = _in_proj(x, norm_mix_pre_g, w_in, l)
        ya = _moba(ka, qa, vt_m, ksum)
        yb = _diff(ka, qa, vt_d, lam_q1[l], lam_k1[l], lam_q2[l], lam_k2[l],
                   diff_subln_g[l], lam_init)
        x1 = _merge(x, ya.reshape(B, MOBA_WIDTH, S), yb.reshape(B, DIFF_WIDTH, S),
                    norm_mix_pre_g, w_in, w_branch_a, w_branch_b, w_out, norm_mix_post_g, l)
        x2 = _ffn(x1.reshape(B * S, D), norm_ffn_pre_g, w_gate.astype(BF16),
                  w_up.astype(BF16), w_down.astype(BF16), norm_ffn_post_g, l)
        x = x2.reshape(B, S, D)
    return x
```

```python
import functools
import math

import numpy as np
import jax
import jax.numpy as jnp
from jax import lax
from jax.experimental import pallas as pl
from jax.experimental.pallas import tpu as pltpu

D_MODEL = 1024
MOBA_HEADS = 8
MOBA_TOPK = 3
MOBA_WIDTH = 512
DIFF_HEADS = 4
DIFF_HEAD_DIM = 64
DIFF_WIDTH = 512
HEAD_DIM = 64
QKV_COLS = 3 * MOBA_WIDTH + 3 * DIFF_WIDTH
D_FF = 2816
NORM_EPS = 1e-6
SUBLN_EPS = 1e-5
ALIBI_MAX_BIAS = 8.0
NEG = -1e30
LOG2E = math.log2(math.e)
Q_SCALE = HEAD_DIM ** -0.5 * LOG2E

LANES = 128
BLK = 256
N_COL_TILES = QKV_COLS // LANES
N_QK_TILES = N_COL_TILES // 3
N_PAIRS = MOBA_HEADS // 2
GROUP_TILES = N_PAIRS
assert DIFF_HEADS == GROUP_TILES and N_QK_TILES == 2 * GROUP_TILES
F32_SUBLANES = 8
BF16_SUBLANES = 16
ONES_ROWS = BF16_SUBLANES
EXTRA_PEN = BF16_SUBLANES
VMEM_LIMIT = 56 * 1024 * 1024

BF16 = jnp.bfloat16
F32 = jnp.float32


def _alibi_slopes():
    n = MOBA_HEADS + DIFF_HEADS
    slopes = 2.0 ** (-ALIBI_MAX_BIAS * np.arange(1, n + 1) / n)
    stride = n // DIFF_HEADS
    diff_idx = np.arange(DIFF_HEADS) * stride + (stride - 1)
    moba_idx = np.setdiff1d(np.arange(n), diff_idx)
    return (np.asarray(slopes[moba_idx], np.float32), np.asarray(slopes[diff_idx], np.float32))


def _split3(v):
    parts, rem = [], float(v)
    for _ in range(3):
        p = float(np.asarray(rem, np.float32).astype(BF16).astype(np.float64))
        parts.append(p)
        rem -= p
    return parts


def _attn_tables(slopes, S):
    nb = S // BLK
    H = len(slopes)
    qx = np.zeros((H, 2, nb, LANES), np.float32)
    kx = np.zeros((2, S, LANES), np.float32)
    pos = np.arange(S)
    for half in range(2):
        e0 = HEAD_DIM * (1 - half)
        for c in range(3):
            kx[half, :, e0 + c] = pos % BLK
            kx[half, :, e0 + 3 + c] = (pos // BLK) * BLK
            kx[half, :, e0 + 6 + c] = 1.0
        for n in range(nb):
            kx[half, n * BLK:(n + 1) * BLK, e0 + EXTRA_PEN + n] = 1.0
        for h in range(H):
            s = float(np.float32(slopes[h])) * LOG2E
            sp = _split3(s)
            for j in range(nb):
                cp = _split3(-s * j * BLK)
                for c in range(3):
                    qx[h, half, j, e0 + c] = sp[c]
                    qx[h, half, j, e0 + 3 + c] = sp[c]
                    qx[h, half, j, e0 + 6 + c] = cp[c]
    return qx.transpose(0, 1, 3, 2), kx


def _query_extras(S, per_step):
    nb = S // BLK
    moba_slopes, diff_slopes = _alibi_slopes()
    qx_m, _ = _attn_tables(moba_slopes, S)
    qx_d, _ = _attn_tables(diff_slopes, S)
    qx_m = qx_m.reshape(N_PAIRS, 2, 2, LANES, nb)[:, np.arange(2), np.arange(2)]
    qx = np.concatenate([qx_m, qx_d], axis=0)
    qx = qx.reshape(N_QK_TILES, 2, LANES, nb // per_step, per_step)
    return np.ascontiguousarray(qx.transpose(3, 0, 1, 2, 4))


def _dot(a, b):
    return jnp.dot(a, b, preferred_element_type=F32)


def _dot_tn(a_t, b):
    return lax.dot_general(a_t, b, (((0,), (0,)), ((), ())), preferred_element_type=F32)


def _rms(x, g, eps):
    return x * lax.rsqrt(jnp.mean(x * x, axis=-1, keepdims=True) + eps) * g


def _layer_spec(block, l, col_block=0):
    index = (l,) + (0,) * (len(block) - 1) + (col_block,)
    return pl.BlockSpec((1,) + tuple(block), lambda *_: index, pipeline_mode=pl.Buffered(1))


def _in_proj_kernel(x_ref, g_ref, w_ref, kx_ref, qx_ref, ka_out, qa_out, vtm_out, vtd_out,
                    ksum_out, *, parts):
    tm = x_ref.shape[1]
    part = tm // parts
    assert part == BLK
    chunk = GROUP_TILES * LANES
    lane = lax.broadcasted_iota(jnp.int32, (1, LANES), 1)
    row = lax.broadcasted_iota(jnp.int32, (LANES, 1), 0)
    in_half = [(lane >= h * HEAD_DIM) & (lane < (h + 1) * HEAD_DIM) for h in range(2)]
    half_rows = [(row >= h * HEAD_DIM) & (row < (h + 1) * HEAD_DIM) for h in range(2)]
    ones = jnp.ones((ONES_ROWS, part), BF16)
    rows = [slice(p * part, (p + 1) * part) for p in range(parts)]
    hs = [_rms(x_ref[0, r, :], g_ref[0], NORM_EPS).astype(BF16) for r in rows]
    for p, (r, h) in enumerate(zip(rows, hs)):
        for c in range(QKV_COLS // chunk):
            acc = _dot(h, w_ref[0, :, c * chunk:(c + 1) * chunk])
            kind, mixer = ("q", "k", "v")[c % 3], c // 3
            for t in range(GROUP_TILES):
                tile = GROUP_TILES * mixer + t
                val = acc[:, t * LANES:(t + 1) * LANES]
                if kind == "k":
                    if mixer == 0:
                        ksum_out[0, 0, t, p:p + 1, :] = jnp.sum(val, axis=0, keepdims=True)
                    val = val.astype(BF16)
                    for half in range(2):
                        ka_out[0, tile, half, r, :] = jnp.where(
                            in_half[half], val, kx_ref[half, r, :])
                elif kind == "q":
                    val_t = (val * Q_SCALE).astype(BF16).T
                    for half in range(2):
                        ext = jnp.broadcast_to(qx_ref[0, tile, half][:, p:p + 1], (LANES, part))
                        qa_out[0, tile, half, :, r] = jnp.where(
                            half_rows[half], val_t, ext.astype(BF16))
                else:
                    val_t = val.astype(BF16).T
                    if mixer == 0:
                        for hh in range(2):
                            vtm_out[0, t, hh, :, r] = jnp.concatenate(
                                [val_t[hh * HEAD_DIM:(hh + 1) * HEAD_DIM], ones], axis=0)
                    else:
                        vtd_out[0, t, :, r] = jnp.concatenate([val_t, ones], axis=0)


def _in_proj(x, g, w_in, l, tm=1024):
    B, S, D = x.shape
    parts = tm // BLK
    kx = jnp.asarray(_attn_tables(_alibi_slopes()[0], S)[1], BF16)
    qx = jnp.asarray(_query_extras(S, parts))
    n, dv = N_QK_TILES, 2 * DIFF_HEAD_DIM
    return pl.pallas_call(
        functools.partial(_in_proj_kernel, parts=parts),
        grid=(B, S // tm),
        in_specs=[
            pl.BlockSpec((1, tm, D), lambda b, i: (b, i, 0)),
            _layer_spec((1, D), l),
            _layer_spec((D, QKV_COLS), l),
            pl.BlockSpec((2, tm, LANES), lambda b, i: (0, i, 0)),
            pl.BlockSpec((1, n, 2, LANES, parts), lambda b, i: (i, 0, 0, 0, 0)),
        ],
        out_specs=[
            pl.BlockSpec((1, n, 2, tm, LANES), lambda b, i: (b, 0, 0, i, 0)),
            pl.BlockSpec((1, n, 2, LANES, tm), lambda b, i: (b, 0, 0, 0, i)),
            pl.BlockSpec((1, N_PAIRS, 2, HEAD_DIM + ONES_ROWS, tm), lambda b, i: (b, 0, 0, 0, i)),
            pl.BlockSpec((1, DIFF_HEADS, dv + ONES_ROWS, tm), lambda b, i: (b, 0, 0, i)),
            pl.BlockSpec((1, 1, N_PAIRS, parts, LANES), lambda b, i: (b, i, 0, 0, 0)),
        ],
        out_shape=[
            jax.ShapeDtypeStruct((B, n, 2, S, LANES), BF16),
            jax.ShapeDtypeStruct((B, n, 2, LANES, S), BF16),
            jax.ShapeDtypeStruct((B, N_PAIRS, 2, HEAD_DIM + ONES_ROWS, S), BF16),
            jax.ShapeDtypeStruct((B, DIFF_HEADS, dv + ONES_ROWS, S), BF16),
            jax.ShapeDtypeStruct((B, S // tm, N_PAIRS, parts, LANES), F32),
        ],
        compiler_params=pltpu.CompilerParams(
            dimension_semantics=("arbitrary", "arbitrary"), vmem_limit_bytes=VMEM_LIMIT),
        name="in_proj",
    )(x, g[:, None], w_in, kx, qx)


def _causal_tile():
    r = np.arange(BLK)
    return np.where(r[:, None] <= r[None, :], 0.0, NEG).astype(np.float32)


_CAUSAL_SPEC = pl.BlockSpec((BLK, BLK), lambda b, h: (0, 0))


def _scores_t(qa_tile, ka_ref, causal_ref, s_ref, m_ref, item_id, par, slot0, j):
    m8 = None
    for t, n in enumerate(range(j, -1, -1)):
        s = _dot(ka_ref[n * BLK:(n + 1) * BLK, :], qa_tile[...])
        if n == j:
            s = s + causal_ref[...]
        s_ref[par, slot0 + t] = s
        t8 = jnp.max(s.reshape(BLK // F32_SUBLANES, F32_SUBLANES, BLK), axis=0)
        m8 = t8 if m8 is None else jnp.maximum(m8, t8)
        yield
    m_ref[item_id] = jnp.broadcast_to(jnp.max(m8, axis=0, keepdims=True), (F32_SUBLANES, BLK))


def _weighted_values_t(s_ref, m_ref, acc_ref, item_id, par, slot0, vt_ref, j):
    acc = None
    m = m_ref[item_id][:1]
    for t, n in enumerate(range(j, -1, -1)):
        p = jnp.exp2(s_ref[par, slot0 + t] - m).astype(BF16)
        d = _dot(vt_ref[:, n * BLK:(n + 1) * BLK], p)
        acc = d if acc is None else acc + d
        yield
    acc_ref[item_id] = acc


def _rounds(nb):
    tail = min(3, nb)
    return [(j,) for j in range(nb - 1, tail - 1, -1)] + [tuple(range(tail - 1, -1, -1))]


def _round_tiles(nb):
    return max(sum(2 * (j + 1) for j in js) for js in _rounds(nb))


def _pipelined(nb, first, second, last, prepare=None):
    one = jnp.minimum(pl.program_id(0) + 1, 1)
    rounds = []
    for js in _rounds(nb):
        items, slot = [], 0
        for j in js:
            for half in range(2):
                items.append(((j, half), slot))
                slot += j + 1
        rounds.append(items)
    if prepare is not None:
        for item, _slot in rounds[0]:
            prepare(item)
    for r in range(len(rounds) + 2):
        def round_body(_, carry, r=r):
            if 0 <= r - 2 < len(rounds):
                for item, _slot in rounds[r - 2]:
                    last(item)
            if prepare is not None and r + 1 < len(rounds):
                for item, _slot in rounds[r + 1]:
                    prepare(item)
            gens = []
            if r < len(rounds):
                gens += [first(item, r % 2, slot0) for item, slot0 in rounds[r]]
            if 0 <= r - 1 < len(rounds):
                gens += [second(item, (r - 1) % 2, slot0) for item, slot0 in rounds[r - 1]]
            while gens:
                for g in list(gens):
                    if next(g, StopIteration) is StopIteration:
                        gens.remove(g)
            return carry

        lax.fori_loop(0, one, round_body, 0)


def _attn_scratch(S, vt_rows):
    nb = S // BLK
    return [
        pltpu.VMEM((2, _round_tiles(nb), BLK, BLK), F32),
        pltpu.VMEM((2 * nb, F32_SUBLANES, BLK), F32),
        pltpu.VMEM((2 * nb, vt_rows, BLK), F32),
    ]


def _tile_spec(shape, off):
    return pl.BlockSpec((1, 1) + shape, lambda b, h: (b, off + h) + (0,) * len(shape))


def _moba_kernel(qa_in, ka_in, vt_in, ksum_in, causal_ref, o_ref, s_ref, m_ref, acc_ref, qa_ref,
                 g_ref):
    S = ka_in.shape[3]
    nb = S // BLK
    lane = lax.broadcasted_iota(jnp.int32, (1, LANES), 1)
    row8 = lax.broadcasted_iota(jnp.int32, (nb, BLK), 0)

    kmean = jnp.concatenate([ksum_in[0, i, 0] for i in range(ksum_in.shape[1])], axis=0)
    kmean = kmean * (1.0 / BLK)
    for hh in range(2):
        in_head = (lane >= hh * HEAD_DIM) & (lane < (hh + 1) * HEAD_DIM)
        km = jnp.where(in_head, kmean, 0.0)
        km_hi = km.astype(BF16)
        km_lo = (km - km_hi.astype(F32)).astype(BF16)
        g2 = _dot(jnp.concatenate([km_hi, km_lo], axis=0), qa_in[0, 0, hh])
        g_ref[hh] = g2[:nb] + g2[nb:]

    n_sel = max(1, min(MOBA_TOPK, nb - 1))

    def prepare(item):
        j, hh = item
        cols = slice(j * BLK, (j + 1) * BLK)
        qa_ref[2 * j + hh] = qa_in[0, 0, hh, :, cols]
        if j > n_sel:
            gj = g_ref[hh, :, cols]
            cnt = jnp.zeros((nb, BLK), F32)
            for m_ in range(j):
                gm = gj[m_:m_ + 1, :]
                beats = (gm > gj) | ((gm == gj) & (m_ < row8))
                cnt = cnt + beats.astype(F32)
            pen = jnp.where((row8 >= j) | (cnt < n_sel), 0.0, NEG)
            off = HEAD_DIM * (1 - hh) + EXTRA_PEN
            pad = jnp.zeros((BF16_SUBLANES - nb, BLK), F32)
            qa_ref[2 * j + hh, off:off + BF16_SUBLANES, :] = (
                jnp.concatenate([pen, pad], axis=0).astype(BF16))

    def scores(item, par, slot0):
        j, hh = item
        return _scores_t(qa_ref.at[2 * j + hh], ka_in.at[0, 0, hh], causal_ref, s_ref, m_ref,
                         2 * j + hh, par, slot0, j)

    def values(item, par, slot0):
        j, hh = item
        return _weighted_values_t(s_ref, m_ref, acc_ref, 2 * j + hh, par, slot0,
                                  vt_in.at[0, 0, hh], j)

    def finish(item):
        j, hh = item
        if hh == 1:
            halves = []
            for h2 in range(2):
                acc = acc_ref[2 * j + h2]
                halves.append(acc[:HEAD_DIM] / acc[HEAD_DIM:HEAD_DIM + 1])
            o_t = jnp.concatenate(halves, axis=0)
            o_ref[0, 0, :, j * BLK:(j + 1) * BLK] = o_t.astype(BF16)

    _pipelined(nb, scores, values, finish, prepare)


def _moba(ka, qa, vt_m, ksum):
    B, _, _, S, _ = ka.shape
    nb = S // BLK
    assert nb <= BF16_SUBLANES
    rows = HEAD_DIM + ONES_ROWS
    return pl.pallas_call(
        _moba_kernel,
        grid=(B, N_PAIRS),
        in_specs=[
            _tile_spec((2, LANES, S), 0), _tile_spec((2, S, LANES), 0),
            _tile_spec((2, rows, S), 0),
            pl.BlockSpec((1, ksum.shape[1], 1) + ksum.shape[3:], lambda b, h: (b, 0, h, 0, 0)),
            _CAUSAL_SPEC,
        ],
        out_specs=pl.BlockSpec((1, 1, LANES, S), lambda b, h: (b, h, 0, 0)),
        out_shape=jax.ShapeDtypeStruct((B, N_PAIRS, LANES, S), BF16),
        scratch_shapes=_attn_scratch(S, rows) + [
            pltpu.VMEM((2 * nb, LANES, BLK), BF16),
            pltpu.VMEM((2, nb, S), F32),
        ],
        compiler_params=pltpu.CompilerParams(
            dimension_semantics=("arbitrary", "arbitrary"), vmem_limit_bytes=VMEM_LIMIT),
        name="moba",
    )(qa, ka, vt_m, ksum, jnp.asarray(_causal_tile()))


def _diff_kernel(qa_in, ka_in, vt_in, causal_ref, lq1_ref, lk1_ref, lq2_ref, lk2_ref, g_ref,
                 o_ref, s_ref, m_ref, acc_ref, *, lam_init):
    S = ka_in.shape[3]
    nb = S // BLK
    dv = 2 * DIFF_HEAD_DIM

    def scores(item, par, slot0):
        j, mp = item
        return _scores_t(qa_in.at[0, 0, mp, :, pl.ds(j * BLK, BLK)], ka_in.at[0, 0, mp],
                         causal_ref, s_ref, m_ref, 2 * j + mp, par, slot0, j)

    def values(item, par, slot0):
        j, mp = item
        return _weighted_values_t(s_ref, m_ref, acc_ref, 2 * j + mp, par, slot0,
                                  vt_in.at[0, 0], j)

    def finish(item):
        j, mp = item
        if mp == 1:
            lam = (jnp.exp(jnp.sum(lq1_ref[...] * lk1_ref[...], axis=-1, keepdims=True))
                   - jnp.exp(jnp.sum(lq2_ref[...] * lk2_ref[...], axis=-1, keepdims=True))
                   + lam_init)
            outs = []
            for m2 in range(2):
                acc = acc_ref[2 * j + m2]
                outs.append(acc[:dv] / acc[dv:dv + 1])
            a = outs[0] - lam * outs[1]
            inv = lax.rsqrt(jnp.mean(a * a, axis=0, keepdims=True) + SUBLN_EPS)
            y = a * inv * g_ref[...] * (1.0 - lam_init)
            o_ref[0, 0, :, j * BLK:(j + 1) * BLK] = y.astype(BF16)

    _pipelined(nb, scores, values, finish)


def _diff(ka, qa, vt_d, lq1, lk1, lq2, lk2, subln_g, lam_init):
    B, _, _, S, _ = ka.shape
    H = DIFF_HEADS
    vec = lambda n: pl.BlockSpec((1, n), lambda b, h: (0, 0))
    d = DIFF_HEAD_DIM
    rows = 2 * d + ONES_ROWS
    return pl.pallas_call(
        functools.partial(_diff_kernel, lam_init=lam_init),
        grid=(B, H),
        in_specs=[
            _tile_spec((2, LANES, S), N_PAIRS), _tile_spec((2, S, LANES), N_PAIRS),
            _tile_spec((rows, S), 0), _CAUSAL_SPEC,
            vec(d), vec(d), vec(d), vec(d),
            pl.BlockSpec((2 * d, 1), lambda b, h: (0, 0)),
        ],
        out_specs=pl.BlockSpec((1, 1, LANES, S), lambda b, h: (b, h, 0, 0)),
        out_shape=jax.ShapeDtypeStruct((B, H, LANES, S), BF16),
        scratch_shapes=_attn_scratch(S, rows),
        compiler_params=pltpu.CompilerParams(
            dimension_semantics=("arbitrary", "arbitrary"), vmem_limit_bytes=VMEM_LIMIT),
        name="diff_attn",
    )(qa, ka, vt_d, jnp.asarray(_causal_tile()), lq1.reshape(1, d), lk1.reshape(1, d),
      lq2.reshape(1, d), lk2.reshape(1, d), subln_g.reshape(2 * d, 1))


def _merge_kernel(x_ref, ya_ref, yb_ref, gpre_ref, wga_ref, wgb_ref, wa_ref, wb_ref, wo_ref,
                  gpost_ref, o_ref, *, parts):
    _, tm, D = x_ref.shape
    rows = [slice(p * tm // parts, (p + 1) * tm // parts) for p in range(parts)]
    hs = [_rms(x_ref[0, r, :], gpre_ref[0], NORM_EPS).astype(BF16) for r in rows]
    for r, h in zip(rows, hs):
        ga = _dot(h, wga_ref[0])
        a = _dot_tn(ya_ref[0, :, r], wa_ref[0])
        merged = jax.nn.sigmoid(ga) * a
        gb = _dot(h, wgb_ref[0])
        b = _dot_tn(yb_ref[0, :, r], wb_ref[0])
        merged = merged + jax.nn.sigmoid(gb) * b
        o = _dot(merged.astype(BF16), wo_ref[0])
        o_ref[0, r, :] = x_ref[0, r, :] + _rms(o, gpost_ref[0], NORM_EPS)


def _merge(x, ya_t, yb_t, gpre, w_in, wa, wb, wo, gpost, l, tm=1024):
    B, S, D = x.shape
    gate_col = QKV_COLS // D
    return pl.pallas_call(
        functools.partial(_merge_kernel, parts=4),
        grid=(B, S // tm),
        in_specs=[
            pl.BlockSpec((1, tm, D), lambda b, i: (b, i, 0)),
            pl.BlockSpec((1, MOBA_WIDTH, tm), lambda b, i: (b, 0, i)),
            pl.BlockSpec((1, DIFF_WIDTH, tm), lambda b, i: (b, 0, i)),
            _layer_spec((1, D), l),
            _layer_spec((D, D), l, gate_col),
            _layer_spec((D, D), l, gate_col + 1),
            _layer_spec((MOBA_WIDTH, D), l),
            _layer_spec((DIFF_WIDTH, D), l),
            _layer_spec((D, D), l),
            _layer_spec((1, D), l),
        ],
        out_specs=pl.BlockSpec((1, tm, D), lambda b, i: (b, i, 0)),
        out_shape=jax.ShapeDtypeStruct((B, S, D), F32),
        compiler_params=pltpu.CompilerParams(
            dimension_semantics=("arbitrary", "arbitrary"), vmem_limit_bytes=VMEM_LIMIT),
        name="merge",
    )(x, ya_t, yb_t, gpre[:, None], w_in, w_in, wa, wb, wo, gpost[:, None])


def _ffn_kernel(x_ref, gpre_ref, wg_ref, wu_ref, wd_ref, gpost_ref, o_ref, a_ref, *, chunk, parts):
    tm = x_ref.shape[0]
    rows = [slice(p * tm // parts, (p + 1) * tm // parts) for p in range(parts)]
    hs = [_rms(x_ref[r, :], gpre_ref[0], NORM_EPS).astype(BF16) for r in rows]
    for c in range(D_FF // chunk):
        cols = slice(c * chunk, (c + 1) * chunk)
        for r, h in zip(rows, hs):
            g = _dot(h, wg_ref[0, :, cols])
            u = _dot(h, wu_ref[0, :, cols])
            a_ref[r, cols] = (jax.nn.silu(g) * u).astype(BF16)
    for r in rows:
        f = _dot(a_ref[r, :], wd_ref[0])
        o_ref[r, :] = x_ref[r, :] + _rms(f, gpost_ref[0], NORM_EPS)


def _ffn(x2, gpre, wg, wu, wd, gpost, l, tm=1024):
    T, D = x2.shape
    return pl.pallas_call(
        functools.partial(_ffn_kernel, chunk=D_FF // 2, parts=4),
        grid=(T // tm,),
        in_specs=[
            pl.BlockSpec((tm, D), lambda i: (i, 0)),
            _layer_spec((1, D), l),
            _layer_spec((D, D_FF), l),
            _layer_spec((D, D_FF), l),
            _layer_spec((D_FF, D), l),
            _layer_spec((1, D), l),
        ],
        out_specs=pl.BlockSpec((tm, D), lambda i: (i, 0)),
        out_shape=jax.ShapeDtypeStruct((T, D), F32),
        scratch_shapes=[pltpu.VMEM((tm, D_FF), BF16)],
        compiler_params=pltpu.CompilerParams(
            dimension_semantics=("arbitrary",), vmem_limit_bytes=VMEM_LIMIT),
        name="ffn",
    )(x2, gpre[:, None], wg, wu, wd, gpost[:, None])


def kernel(x, norm_mix_pre_g, w_in, w_branch_a, w_branch_b, lam_q1, lam_k1, lam_q2, lam_k2,
           diff_subln_g, w_out, norm_mix_post_g, norm_ffn_pre_g, w_gate, w_up, w_down,
           norm_ffn_post_g):
    B, S, D = x.shape
    assert D == D_MODEL and S % BLK == 0
    depth = w_in.shape[0]
    for l in range(depth):
        lam_init = 0.8 - 0.6 * math.exp(-0.3 * l)
        ka, qa, vt_m, vt_d, ksum = _in_proj(x, norm_mix_pre_g, w_in, l)
        ya = _moba(ka, qa, vt_m, ksum)
        yb = _diff(ka, qa, vt_d, lam_q1[l], lam_k1[l], lam_q2[l], lam_k2[l],
                   diff_subln_g[l], lam_init)
        x1 = _merge(x, ya.reshape(B, MOBA_WIDTH, S), yb.reshape(B, DIFF_WIDTH, S),
                    norm_mix_pre_g, w_in, w_branch_a, w_branch_b, w_out, norm_mix_post_g, l)
        x2 = _ffn(x1.reshape(B * S, D), norm_ffn_pre_g, w_gate.astype(BF16),
                  w_up.astype(BF16), w_down.astype(BF16), norm_ffn_post_g, l)
        x = x2.reshape(B, S, D)
    return x
```

```python
import functools
import math

import numpy as np
import jax
import jax.numpy as jnp
from jax import lax
from jax.experimental import pallas as pl
from jax.experimental.pallas import tpu as pltpu

D_MODEL = 1024
MOBA_HEADS = 8
MOBA_TOPK = 3
MOBA_WIDTH = 512
DIFF_HEADS = 4
DIFF_HEAD_DIM = 64
DIFF_WIDTH = 512
HEAD_DIM = 64
QKV_COLS = 3 * MOBA_WIDTH + 3 * DIFF_WIDTH
D_FF = 2816
NORM_EPS = 1e-6
SUBLN_EPS = 1e-5
ALIBI_MAX_BIAS = 8.0
NEG = -1e30
LOG2E = math.log2(math.e)
Q_SCALE = HEAD_DIM ** -0.5 * LOG2E

LANES = 128
BLK = 256
N_COL_TILES = QKV_COLS // LANES
N_QK_TILES = N_COL_TILES // 3
N_PAIRS = MOBA_HEADS // 2
GROUP_TILES = N_PAIRS
assert DIFF_HEADS == GROUP_TILES and N_QK_TILES == 2 * GROUP_TILES
F32_SUBLANES = 8
BF16_SUBLANES = 16
ONES_ROWS = BF16_SUBLANES
EXTRA_PEN = BF16_SUBLANES
UNITS = 2
VMEM_LIMIT = 56 * 1024 * 1024

BF16 = jnp.bfloat16
F32 = jnp.float32


def _alibi_slopes():
    n = MOBA_HEADS + DIFF_HEADS
    slopes = 2.0 ** (-ALIBI_MAX_BIAS * np.arange(1, n + 1) / n)
    stride = n // DIFF_HEADS
    diff_idx = np.arange(DIFF_HEADS) * stride + (stride - 1)
    moba_idx = np.setdiff1d(np.arange(n), diff_idx)
    return (np.asarray(slopes[moba_idx], np.float32), np.asarray(slopes[diff_idx], np.float32))


def _split3(v):
    parts, rem = [], float(v)
    for _ in range(3):
        p = float(np.asarray(rem, np.float32).astype(BF16).astype(np.float64))
        parts.append(p)
        rem -= p
    return parts


def _attn_tables(slopes, S):
    nb = S // BLK
    H = len(slopes)
    qx = np.zeros((H, 2, nb, LANES), np.float32)
    kx = np.zeros((2, S, LANES), np.float32)
    pos = np.arange(S)
    for half in range(2):
        e0 = HEAD_DIM * (1 - half)
        for c in range(3):
            kx[half, :, e0 + c] = pos % BLK
            kx[half, :, e0 + 3 + c] = (pos // BLK) * BLK
            kx[half, :, e0 + 6 + c] = 1.0
        for n in range(nb):
            kx[half, n * BLK:(n + 1) * BLK, e0 + EXTRA_PEN + n] = 1.0
        for h in range(H):
            s = float(np.float32(slopes[h])) * LOG2E
            sp = _split3(s)
            for j in range(nb):
                cp = _split3(-s * j * BLK)
                for c in range(3):
                    qx[h, half, j, e0 + c] = sp[c]
                    qx[h, half, j, e0 + 3 + c] = sp[c]
                    qx[h, half, j, e0 + 6 + c] = cp[c]
    return qx.transpose(0, 1, 3, 2), kx


def _query_extras(S, per_step):
    nb = S // BLK
    moba_slopes, diff_slopes = _alibi_slopes()
    qx_m, _ = _attn_tables(moba_slopes, S)
    qx_d, _ = _attn_tables(diff_slopes, S)
    qx_m = qx_m.reshape(N_PAIRS, 2, 2, LANES, nb)[:, np.arange(2), np.arange(2)]
    qx = np.concatenate([qx_m, qx_d], axis=0)
    qx = qx.reshape(N_QK_TILES, 2, LANES, nb // per_step, per_step)
    return np.ascontiguousarray(qx.transpose(3, 0, 1, 2, 4))


def _dot(a, b):
    return jnp.dot(a, b, preferred_element_type=F32)


def _dot_tn(a_t, b):
    return lax.dot_general(a_t, b, (((0,), (0,)), ((), ())), preferred_element_type=F32)


def _rms(x, g, eps):
    return x * lax.rsqrt(jnp.mean(x * x, axis=-1, keepdims=True) + eps) * g


def _layer_spec(block, l, col_block=0):
    index = (l,) + (0,) * (len(block) - 1) + (col_block,)
    return pl.BlockSpec((1,) + tuple(block), lambda *_: index, pipeline_mode=pl.Buffered(1))


def _in_proj_kernel(x_ref, g_ref, w_ref, kx_ref, qx_ref, ka_out, qa_out, vtm_out, vtd_out,
                    ksum_out, *, parts):
    tm = x_ref.shape[1]
    part = tm // parts
    assert part == BLK
    chunk = GROUP_TILES * LANES
    lane = lax.broadcasted_iota(jnp.int32, (1, LANES), 1)
    row = lax.broadcasted_iota(jnp.int32, (LANES, 1), 0)
    in_half = [(lane >= h * HEAD_DIM) & (lane < (h + 1) * HEAD_DIM) for h in range(2)]
    half_rows = [(row >= h * HEAD_DIM) & (row < (h + 1) * HEAD_DIM) for h in range(2)]
    ones = jnp.ones((ONES_ROWS, part), BF16)
    rows = [slice(p * part, (p + 1) * part) for p in range(parts)]
    hs = [_rms(x_ref[0, r, :], g_ref[0], NORM_EPS).astype(BF16) for r in rows]
    for p, (r, h) in enumerate(zip(rows, hs)):
        for c in range(QKV_COLS // chunk):
            acc = _dot(h, w_ref[0, :, c * chunk:(c + 1) * chunk])
            kind, mixer = ("q", "k", "v")[c % 3], c // 3
            for t in range(GROUP_TILES):
                tile = GROUP_TILES * mixer + t
                val = acc[:, t * LANES:(t + 1) * LANES]
                if kind == "k":
                    if mixer == 0:
                        ksum_out[0, 0, t, p:p + 1, :] = jnp.sum(val, axis=0, keepdims=True)
                    val = val.astype(BF16)
                    for half in range(2):
                        ka_out[0, tile, half, r, :] = jnp.where(
                            in_half[half], val, kx_ref[half, r, :])
                elif kind == "q":
                    val_t = (val * Q_SCALE).astype(BF16).T
                    for half in range(2):
                        ext = jnp.broadcast_to(qx_ref[0, tile, half][:, p:p + 1], (LANES, part))
                        qa_out[0, tile, half, :, r] = jnp.where(
                            half_rows[half], val_t, ext.astype(BF16))
                else:
                    val_t = val.astype(BF16).T
                    if mixer == 0:
                        for hh in range(2):
                            vtm_out[0, t, hh, :, r] = jnp.concatenate(
                                [val_t[hh * HEAD_DIM:(hh + 1) * HEAD_DIM], ones], axis=0)
                    else:
                        vtd_out[0, t, :, r] = jnp.concatenate([val_t, ones], axis=0)


def _in_proj(x, g, w_in, l, tm=1024):
    B, S, D = x.shape
    parts = tm // BLK
    kx = jnp.asarray(_attn_tables(_alibi_slopes()[0], S)[1], BF16)
    qx = jnp.asarray(_query_extras(S, parts))
    n, dv = N_QK_TILES, 2 * DIFF_HEAD_DIM
    return pl.pallas_call(
        functools.partial(_in_proj_kernel, parts=parts),
        grid=(B, S // tm),
        in_specs=[
            pl.BlockSpec((1, tm, D), lambda b, i: (b, i, 0)),
            _layer_spec((1, D), l),
            _layer_spec((D, QKV_COLS), l),
            pl.BlockSpec((2, tm, LANES), lambda b, i: (0, i, 0)),
            pl.BlockSpec((1, n, 2, LANES, parts), lambda b, i: (i, 0, 0, 0, 0)),
        ],
        out_specs=[
            pl.BlockSpec((1, n, 2, tm, LANES), lambda b, i: (b, 0, 0, i, 0)),
            pl.BlockSpec((1, n, 2, LANES, tm), lambda b, i: (b, 0, 0, 0, i)),
            pl.BlockSpec((1, N_PAIRS, 2, HEAD_DIM + ONES_ROWS, tm), lambda b, i: (b, 0, 0, 0, i)),
            pl.BlockSpec((1, DIFF_HEADS, dv + ONES_ROWS, tm), lambda b, i: (b, 0, 0, i)),
            pl.BlockSpec((1, 1, N_PAIRS, parts, LANES), lambda b, i: (b, i, 0, 0, 0)),
        ],
        out_shape=[
            jax.ShapeDtypeStruct((B, n, 2, S, LANES), BF16),
            jax.ShapeDtypeStruct((B, n, 2, LANES, S), BF16),
            jax.ShapeDtypeStruct((B, N_PAIRS, 2, HEAD_DIM + ONES_ROWS, S), BF16),
            jax.ShapeDtypeStruct((B, DIFF_HEADS, dv + ONES_ROWS, S), BF16),
            jax.ShapeDtypeStruct((B, S // tm, N_PAIRS, parts, LANES), F32),
        ],
        compiler_params=pltpu.CompilerParams(
            dimension_semantics=("arbitrary", "arbitrary"), vmem_limit_bytes=VMEM_LIMIT),
        name="in_proj",
    )(x, g[:, None], w_in, kx, qx)


def _causal_tile():
    r = np.arange(BLK)
    return np.where(r[:, None] <= r[None, :], 0.0, NEG).astype(np.float32)


_CAUSAL_SPEC = pl.BlockSpec((BLK, BLK), lambda b, h: (0, 0))


def _scores_t(qa_tile, ka_ref, causal_ref, s_ref, m_ref, item_id, par, slot0, j):
    m8 = None
    for t, n in enumerate(range(j, -1, -1)):
        s = _dot(ka_ref[n * BLK:(n + 1) * BLK, :], qa_tile[...])
        if n == j:
            s = s + causal_ref[...]
        s_ref[par, slot0 + t] = s
        t8 = jnp.max(s.reshape(BLK // F32_SUBLANES, F32_SUBLANES, BLK), axis=0)
        m8 = t8 if m8 is None else jnp.maximum(m8, t8)
        yield
    m_ref[item_id] = jnp.broadcast_to(jnp.max(m8, axis=0, keepdims=True), (F32_SUBLANES, BLK))


def _weighted_values_t(s_ref, m_ref, acc_ref, item_id, par, slot0, vt_ref, j):
    acc = None
    m = m_ref[item_id][:1]
    for t, n in enumerate(range(j, -1, -1)):
        p = jnp.exp2(s_ref[par, slot0 + t] - m).astype(BF16)
        d = _dot(vt_ref[:, n * BLK:(n + 1) * BLK], p)
        acc = d if acc is None else acc + d
        yield
    acc_ref[item_id] = acc


def _rounds(nb):
    tail = min(3, nb)
    return [(j,) for j in range(nb - 1, tail - 1, -1)] + [tuple(range(tail - 1, -1, -1))]


def _round_tiles(nb):
    return max(sum(2 * (j + 1) for j in js) for js in _rounds(nb))


def _pipelined(nb, units, first, second, last, prepare=None):
    one = jnp.minimum(pl.program_id(0) + 1, 1)
    rounds = []
    for js in _rounds(nb):
        items = []
        for u in range(units):
            slot = 0
            for j in js:
                for half in range(2):
                    items.append(((u, j, half), slot))
                    slot += j + 1
        rounds.append(items)
    if prepare is not None:
        for item, _slot in rounds[0]:
            prepare(item)
    for r in range(len(rounds) + 2):
        def round_body(_, carry, r=r):
            if 0 <= r - 2 < len(rounds):
                for item, _slot in rounds[r - 2]:
                    last(item)
            if prepare is not None and r + 1 < len(rounds):
                for item, _slot in rounds[r + 1]:
                    prepare(item)
            gens = []
            if r < len(rounds):
                gens += [first(item, r % 2, slot0) for item, slot0 in rounds[r]]
            if 0 <= r - 1 < len(rounds):
                gens += [second(item, (r - 1) % 2, slot0) for item, slot0 in rounds[r - 1]]
            while gens:
                for g in list(gens):
                    if next(g, StopIteration) is StopIteration:
                        gens.remove(g)
            return carry

        lax.fori_loop(0, one, round_body, 0)


def _attn_scratch(S, vt_rows):
    nb = S // BLK
    return [
        pltpu.VMEM((UNITS, 2, _round_tiles(nb), BLK, BLK), F32),
        pltpu.VMEM((UNITS, 2 * nb, F32_SUBLANES, BLK), F32),
        pltpu.VMEM((UNITS, 2 * nb, vt_rows, BLK), F32),
    ]


def _tile_spec(shape, off):
    return pl.BlockSpec((UNITS, 1) + shape, lambda b, h: (b, off + h) + (0,) * len(shape))


def _moba_kernel(qa_in, ka_in, vt_in, ksum_in, causal_ref, o_ref, s_ref, m_ref, acc_ref, qa_ref,
                 g_ref):
    S = ka_in.shape[3]
    nb = S // BLK
    lane = lax.broadcasted_iota(jnp.int32, (1, LANES), 1)
    row8 = lax.broadcasted_iota(jnp.int32, (nb, BLK), 0)

    for u in range(UNITS):
        kmean = jnp.concatenate([ksum_in[u, i, 0] for i in range(ksum_in.shape[1])], axis=0)
        kmean = kmean * (1.0 / BLK)
        for hh in range(2):
            in_head = (lane >= hh * HEAD_DIM) & (lane < (hh + 1) * HEAD_DIM)
            km = jnp.where(in_head, kmean, 0.0)
            km_hi = km.astype(BF16)
            km_lo = (km - km_hi.astype(F32)).astype(BF16)
            g2 = _dot(jnp.concatenate([km_hi, km_lo], axis=0), qa_in[u, 0, hh])
            g_ref[u, hh] = g2[:nb] + g2[nb:]

    n_sel = max(1, min(MOBA_TOPK, nb - 1))

    def prepare(item):
        u, j, hh = item
        cols = slice(j * BLK, (j + 1) * BLK)
        qa_ref[u, 2 * j + hh] = qa_in[u, 0, hh, :, cols]
        if j > n_sel:
            gj = g_ref[u, hh, :, cols]
            cnt = jnp.zeros((nb, BLK), F32)
            for m_ in range(j):
                gm = gj[m_:m_ + 1, :]
                beats = (gm > gj) | ((gm == gj) & (m_ < row8))
                cnt = cnt + beats.astype(F32)
            pen = jnp.where((row8 >= j) | (cnt < n_sel), 0.0, NEG)
            off = HEAD_DIM * (1 - hh) + EXTRA_PEN
            pad = jnp.zeros((BF16_SUBLANES - nb, BLK), F32)
            qa_ref[u, 2 * j + hh, off:off + BF16_SUBLANES, :] = (
                jnp.concatenate([pen, pad], axis=0).astype(BF16))

    def scores(item, par, slot0):
        u, j, hh = item
        return _scores_t(qa_ref.at[u, 2 * j + hh], ka_in.at[u, 0, hh], causal_ref, s_ref.at[u],
                         m_ref.at[u], 2 * j + hh, par, slot0, j)

    def values(item, par, slot0):
        u, j, hh = item
        return _weighted_values_t(s_ref.at[u], m_ref.at[u], acc_ref.at[u], 2 * j + hh, par,
                                  slot0, vt_in.at[u, 0, hh], j)

    def finish(item):
        u, j, hh = item
        if hh == 1:
            halves = []
            for h2 in range(2):
                acc = acc_ref[u, 2 * j + h2]
                halves.append(acc[:HEAD_DIM] / acc[HEAD_DIM:HEAD_DIM + 1])
            o_t = jnp.concatenate(halves, axis=0)
            o_ref[u, 0, :, j * BLK:(j + 1) * BLK] = o_t.astype(BF16)

    _pipelined(nb, UNITS, scores, values, finish, prepare)


def _moba(ka, qa, vt_m, ksum):
    B, _, _, S, _ = ka.shape
    nb = S // BLK
    assert nb <= BF16_SUBLANES
    assert B % UNITS == 0
    rows = HEAD_DIM + ONES_ROWS
    return pl.pallas_call(
        _moba_kernel,
        grid=(B // UNITS, N_PAIRS),
        in_specs=[
            _tile_spec((2, LANES, S), 0), _tile_spec((2, S, LANES), 0),
            _tile_spec((2, rows, S), 0),
            pl.BlockSpec((UNITS, ksum.shape[1], 1) + ksum.shape[3:],
                         lambda b, h: (b, 0, h, 0, 0)),
            _CAUSAL_SPEC,
        ],
        out_specs=pl.BlockSpec((UNITS, 1, LANES, S), lambda b, h: (b, h, 0, 0)),
        out_shape=jax.ShapeDtypeStruct((B, N_PAIRS, LANES, S), BF16),
        scratch_shapes=_attn_scratch(S, rows) + [
            pltpu.VMEM((UNITS, 2 * nb, LANES, BLK), BF16),
            pltpu.VMEM((UNITS, 2, nb, S), F32),
        ],
        compiler_params=pltpu.CompilerParams(
            dimension_semantics=("arbitrary", "arbitrary"), vmem_limit_bytes=VMEM_LIMIT),
        name="moba",
    )(qa, ka, vt_m, ksum, jnp.asarray(_causal_tile()))


def _diff_kernel(qa_in, ka_in, vt_in, causal_ref, lq1_ref, lk1_ref, lq2_ref, lk2_ref, g_ref,
                 o_ref, s_ref, m_ref, acc_ref, *, lam_init):
    S = ka_in.shape[3]
    nb = S // BLK
    dv = 2 * DIFF_HEAD_DIM

    def scores(item, par, slot0):
        u, j, mp = item
        return _scores_t(qa_in.at[u, 0, mp, :, pl.ds(j * BLK, BLK)], ka_in.at[u, 0, mp],
                         causal_ref, s_ref.at[u], m_ref.at[u], 2 * j + mp, par, slot0, j)

    def values(item, par, slot0):
        u, j, mp = item
        return _weighted_values_t(s_ref.at[u], m_ref.at[u], acc_ref.at[u], 2 * j + mp, par,
                                  slot0, vt_in.at[u, 0], j)

    def finish(item):
        u, j, mp = item
        if mp == 1:
            lam = (jnp.exp(jnp.sum(lq1_ref[...] * lk1_ref[...], axis=-1, keepdims=True))
                   - jnp.exp(jnp.sum(lq2_ref[...] * lk2_ref[...], axis=-1, keepdims=True))
                   + lam_init)
            outs = []
            for m2 in range(2):
                acc = acc_ref[u, 2 * j + m2]
                outs.append(acc[:dv] / acc[dv:dv + 1])
            a = outs[0] - lam * outs[1]
            inv = lax.rsqrt(jnp.mean(a * a, axis=0, keepdims=True) + SUBLN_EPS)
            y = a * inv * g_ref[...] * (1.0 - lam_init)
            o_ref[u, 0, :, j * BLK:(j + 1) * BLK] = y.astype(BF16)

    _pipelined(nb, UNITS, scores, values, finish)


def _diff(ka, qa, vt_d, lq1, lk1, lq2, lk2, subln_g, lam_init):
    B, _, _, S, _ = ka.shape
    assert B % UNITS == 0
    H = DIFF_HEADS
    vec = lambda n: pl.BlockSpec((1, n), lambda b, h: (0, 0))
    d = DIFF_HEAD_DIM
    rows = 2 * d + ONES_ROWS
    return pl.pallas_call(
        functools.partial(_diff_kernel, lam_init=lam_init),
        grid=(B // UNITS, H),
        in_specs=[
            _tile_spec((2, LANES, S), N_PAIRS), _tile_spec((2, S, LANES), N_PAIRS),
            _tile_spec((rows, S), 0), _CAUSAL_SPEC,
            vec(d), vec(d), vec(d), vec(d),
            pl.BlockSpec((2 * d, 1), lambda b, h: (0, 0)),
        ],
        out_specs=pl.BlockSpec((UNITS, 1, LANES, S), lambda b, h: (b, h, 0, 0)),
        out_shape=jax.ShapeDtypeStruct((B, H, LANES, S), BF16),
        scratch_shapes=_attn_scratch(S, rows),
        compiler_params=pltpu.CompilerParams(
            dimension_semantics=("arbitrary", "arbitrary"), vmem_limit_bytes=VMEM_LIMIT),
        name="diff_attn",
    )(qa, ka, vt_d, jnp.asarray(_causal_tile()), lq1.reshape(1, d), lk1.reshape(1, d),
      lq2.reshape(1, d), lk2.reshape(1, d), subln_g.reshape(2 * d, 1))


def _merge_kernel(x_ref, ya_ref, yb_ref, gpre_ref, wga_ref, wgb_ref, wa_ref, wb_ref, wo_ref,
                  gpost_ref, o_ref, *, parts):
    _, tm, D = x_ref.shape
    rows = [slice(p * tm // parts, (p + 1) * tm // parts) for p in range(parts)]
    hs = [_rms(x_ref[0, r, :], gpre_ref[0], NORM_EPS).astype(BF16) for r in rows]
    for r, h in zip(rows, hs):
        ga = _dot(h, wga_ref[0])
        a = _dot_tn(ya_ref[0, :, r], wa_ref[0])
        merged = jax.nn.sigmoid(ga) * a
        gb = _dot(h, wgb_ref[0])
        b = _dot_tn(yb_ref[0, :, r], wb_ref[0])
        merged = merged + jax.nn.sigmoid(gb) * b
        o = _dot(merged.astype(BF16), wo_ref[0])
        o_ref[0, r, :] = x_ref[0, r, :] + _rms(o, gpost_ref[0], NORM_EPS)


def _merge(x, ya_t, yb_t, gpre, w_in, wa, wb, wo, gpost, l, tm=1024):
    B, S, D = x.shape
    gate_col = QKV_COLS // D
    return pl.pallas_call(
        functools.partial(_merge_kernel, parts=4),
        grid=(B, S // tm),
        in_specs=[
            pl.BlockSpec((1, tm, D), lambda b, i: (b, i, 0)),
            pl.BlockSpec((1, MOBA_WIDTH, tm), lambda b, i: (b, 0, i)),
            pl.BlockSpec((1, DIFF_WIDTH, tm), lambda b, i: (b, 0, i)),
            _layer_spec((1, D), l),
            _layer_spec((D, D), l, gate_col),
            _layer_spec((D, D), l, gate_col + 1),
            _layer_spec((MOBA_WIDTH, D), l),
            _layer_spec((DIFF_WIDTH, D), l),
            _layer_spec((D, D), l),
            _layer_spec((1, D), l),
        ],
        out_specs=pl.BlockSpec((1, tm, D), lambda b, i: (b, i, 0)),
        out_shape=jax.ShapeDtypeStruct((B, S, D), F32),
        compiler_params=pltpu.CompilerParams(
            dimension_semantics=("arbitrary", "arbitrary"), vmem_limit_bytes=VMEM_LIMIT),
        name="merge",
    )(x, ya_t, yb_t, gpre[:, None], w_in, w_in, wa, wb, wo, gpost[:, None])


def _ffn_kernel(x_ref, gpre_ref, wg_ref, wu_ref, wd_ref, gpost_ref, o_ref, a_ref, *, chunk, parts):
    tm = x_ref.shape[0]
    rows = [slice(p * tm // parts, (p + 1) * tm // parts) for p in range(parts)]
    hs = [_rms(x_ref[r, :], gpre_ref[0], NORM_EPS).astype(BF16) for r in rows]
    for c in range(D_FF // chunk):
        cols = slice(c * chunk, (c + 1) * chunk)
        for r, h in zip(rows, hs):
            g = _dot(h, wg_ref[0, :, cols])
            u = _dot(h, wu_ref[0, :, cols])
            a_ref[r, cols] = (jax.nn.silu(g) * u).astype(BF16)
    for r in rows:
        f = _dot(a_ref[r, :], wd_ref[0])
        o_ref[r, :] = x_ref[r, :] + _rms(f, gpost_ref[0], NORM_EPS)


def _ffn(x2, gpre, wg, wu, wd, gpost, l, tm=1024):
    T, D = x2.shape
    return pl.pallas_call(
        functools.partial(_ffn_kernel, chunk=D_FF // 2, parts=4),
        grid=(T // tm,),
        in_specs=[
            pl.BlockSpec((tm, D), lambda i: (i, 0)),
            _layer_spec((1, D), l),
            _layer_spec((D, D_FF), l),
            _layer_spec((D, D_FF), l),
            _layer_spec((D_FF, D), l),
            _layer_spec((1, D), l),
        ],
        out_specs=pl.BlockSpec((tm, D), lambda i: (i, 0)),
        out_shape=jax.ShapeDtypeStruct((T, D), F32),
        scratch_shapes=[pltpu.VMEM((tm, D_FF), BF16)],
        compiler_params=pltpu.CompilerParams(
            dimension_semantics=("arbitrary",), vmem_limit_bytes=VMEM_LIMIT),
        name="ffn",
    )(x2, gpre[:, None], wg, wu, wd, gpost[:, None])


def kernel(x, norm_mix_pre_g, w_in, w_branch_a, w_branch_b, lam_q1, lam_k1, lam_q2, lam_k2,
           diff_subln_g, w_out, norm_mix_post_g, norm_ffn_pre_g, w_gate, w_up, w_down,
           norm_ffn_post_g):
    B, S, D = x.shape
    assert D == D_MODEL and S % BLK == 0
    depth = w_in.shape[0]
    for l in range(depth):
        lam_init = 0.8 - 0.6 * math.exp(-0.3 * l)
        ka, qa, vt_m, vt_d, ksum = _in_proj(x, norm_mix_pre_g, w_in, l)
        ya = _moba(ka, qa, vt_m, ksum)
        yb = _diff(ka, qa, vt_d, lam_q1[l], lam_k1[l], lam_q2[l], lam_k2[l],
                   diff_subln_g[l], lam_init)
        x1 = _merge(x, ya.reshape(B, MOBA_WIDTH, S), yb.reshape(B, DIFF_WIDTH, S),
                    norm_mix_pre_g, w_in, w_branch_a, w_branch_b, w_out, norm_mix_post_g, l)
        x2 = _ffn(x1.reshape(B * S, D), norm_ffn_pre_g, w_gate.astype(BF16),
                  w_up.astype(BF16), w_down.astype(BF16), norm_ffn_post_g, l)
        x = x2.reshape(B, S, D)
    return x
```

```python
import functools
import math

import numpy as np
import jax
import jax.numpy as jnp
from jax import lax
from jax.experimental import pallas as pl
from jax.experimental.pallas import tpu as pltpu

D_MODEL = 1024
MOBA_HEADS = 8
MOBA_TOPK = 3
MOBA_WIDTH = 512
DIFF_HEADS = 4
DIFF_HEAD_DIM = 64
DIFF_WIDTH = 512
HEAD_DIM = 64
QKV_COLS = 3 * MOBA_WIDTH + 3 * DIFF_WIDTH
D_FF = 2816
NORM_EPS = 1e-6
SUBLN_EPS = 1e-5
ALIBI_MAX_BIAS = 8.0
NEG = -1e30
LOG2E = math.log2(math.e)
Q_SCALE = HEAD_DIM ** -0.5 * LOG2E

LANES = 128
BLK = 256
N_COL_TILES = QKV_COLS // LANES
N_QK_TILES = N_COL_TILES // 3
N_PAIRS = MOBA_HEADS // 2
GROUP_TILES = N_PAIRS
assert DIFF_HEADS == GROUP_TILES and N_QK_TILES == 2 * GROUP_TILES
F32_SUBLANES = 8
BF16_SUBLANES = 16
ONES_ROWS = BF16_SUBLANES
EXTRA_PEN = BF16_SUBLANES
UNITS = 2
VMEM_LIMIT = 56 * 1024 * 1024

BF16 = jnp.bfloat16
F32 = jnp.float32


def _alibi_slopes():
    n = MOBA_HEADS + DIFF_HEADS
    slopes = 2.0 ** (-ALIBI_MAX_BIAS * np.arange(1, n + 1) / n)
    stride = n // DIFF_HEADS
    diff_idx = np.arange(DIFF_HEADS) * stride + (stride - 1)
    moba_idx = np.setdiff1d(np.arange(n), diff_idx)
    return (np.asarray(slopes[moba_idx], np.float32), np.asarray(slopes[diff_idx], np.float32))


def _split3(v):
    parts, rem = [], float(v)
    for _ in range(3):
        p = float(np.asarray(rem, np.float32).astype(BF16).astype(np.float64))
        parts.append(p)
        rem -= p
    return parts


def _attn_tables(slopes, S):
    nb = S // BLK
    H = len(slopes)
    qx = np.zeros((H, 2, nb, LANES), np.float32)
    kx = np.zeros((2, S, LANES), np.float32)
    pos = np.arange(S)
    for half in range(2):
        e0 = HEAD_DIM * (1 - half)
        for c in range(3):
            kx[half, :, e0 + c] = pos % BLK
            kx[half, :, e0 + 3 + c] = (pos // BLK) * BLK
            kx[half, :, e0 + 6 + c] = 1.0
        for n in range(nb):
            kx[half, n * BLK:(n + 1) * BLK, e0 + EXTRA_PEN + n] = 1.0
        for h in range(H):
            s = float(np.float32(slopes[h])) * LOG2E
            sp = _split3(s)
            for j in range(nb):
                cp = _split3(-s * j * BLK)
                for c in range(3):
                    qx[h, half, j, e0 + c] = sp[c]
                    qx[h, half, j, e0 + 3 + c] = sp[c]
                    qx[h, half, j, e0 + 6 + c] = cp[c]
    return qx.transpose(0, 1, 3, 2), kx


def _query_extras(S, per_step):
    nb = S // BLK
    moba_slopes, diff_slopes = _alibi_slopes()
    qx_m, _ = _attn_tables(moba_slopes, S)
    qx_d, _ = _attn_tables(diff_slopes, S)
    qx_m = qx_m.reshape(N_PAIRS, 2, 2, LANES, nb)[:, np.arange(2), np.arange(2)]
    qx = np.concatenate([qx_m, qx_d], axis=0)
    qx = qx.reshape(N_QK_TILES, 2, LANES, nb // per_step, per_step)
    return np.ascontiguousarray(qx.transpose(3, 0, 1, 2, 4))


def _dot(a, b):
    return jnp.dot(a, b, preferred_element_type=F32)


def _dot_tn(a_t, b):
    return lax.dot_general(a_t, b, (((0,), (0,)), ((), ())), preferred_element_type=F32)


def _rms(x, g, eps):
    return x * lax.rsqrt(jnp.mean(x * x, axis=-1, keepdims=True) + eps) * g


def _layer_spec(block, l, col_block=0):
    index = (l,) + (0,) * (len(block) - 1) + (col_block,)
    return pl.BlockSpec((1,) + tuple(block), lambda *_: index, pipeline_mode=pl.Buffered(1))


def _in_proj_kernel(x_ref, g_ref, w_ref, kx_ref, qx_ref, ka_out, qa_out, vtm_out, vtd_out,
                    ksum_out, *, parts):
    tm = x_ref.shape[1]
    part = tm // parts
    assert part == BLK
    chunk = GROUP_TILES * LANES
    lane = lax.broadcasted_iota(jnp.int32, (1, LANES), 1)
    row = lax.broadcasted_iota(jnp.int32, (LANES, 1), 0)
    in_half = [(lane >= h * HEAD_DIM) & (lane < (h + 1) * HEAD_DIM) for h in range(2)]
    half_rows = [(row >= h * HEAD_DIM) & (row < (h + 1) * HEAD_DIM) for h in range(2)]
    ones = jnp.ones((ONES_ROWS, part), BF16)
    rows = [slice(p * part, (p + 1) * part) for p in range(parts)]
    hs = [_rms(x_ref[0, r, :], g_ref[0], NORM_EPS).astype(BF16) for r in rows]
    for p, (r, h) in enumerate(zip(rows, hs)):
        for c in range(QKV_COLS // chunk):
            acc = _dot(h, w_ref[0, :, c * chunk:(c + 1) * chunk])
            kind, mixer = ("q", "k", "v")[c % 3], c // 3
            for t in range(GROUP_TILES):
                tile = GROUP_TILES * mixer + t
                val = acc[:, t * LANES:(t + 1) * LANES]
                if kind == "k":
                    if mixer == 0:
                        ksum_out[0, 0, t, p:p + 1, :] = jnp.sum(val, axis=0, keepdims=True)
                    val = val.astype(BF16)
                    for half in range(2):
                        ka_out[0, tile, half, r, :] = jnp.where(
                            in_half[half], val, kx_ref[half, r, :])
                elif kind == "q":
                    val_t = (val * Q_SCALE).astype(BF16).T
                    for half in range(2):
                        ext = jnp.broadcast_to(qx_ref[0, tile, half][:, p:p + 1], (LANES, part))
                        qa_out[0, tile, half, :, r] = jnp.where(
                            half_rows[half], val_t, ext.astype(BF16))
                else:
                    val_t = val.astype(BF16).T
                    if mixer == 0:
                        for hh in range(2):
                            vtm_out[0, t, hh, :, r] = jnp.concatenate(
                                [val_t[hh * HEAD_DIM:(hh + 1) * HEAD_DIM], ones], axis=0)
                    else:
                        vtd_out[0, t, :, r] = jnp.concatenate([val_t, ones], axis=0)


def _in_proj(x, g, w_in, l, tm=1024):
    B, S, D = x.shape
    parts = tm // BLK
    kx = jnp.asarray(_attn_tables(_alibi_slopes()[0], S)[1], BF16)
    qx = jnp.asarray(_query_extras(S, parts))
    n, dv = N_QK_TILES, 2 * DIFF_HEAD_DIM
    return pl.pallas_call(
        functools.partial(_in_proj_kernel, parts=parts),
        grid=(B, S // tm),
        in_specs=[
            pl.BlockSpec((1, tm, D), lambda b, i: (b, i, 0)),
            _layer_spec((1, D), l),
            _layer_spec((D, QKV_COLS), l),
            pl.BlockSpec((2, tm, LANES), lambda b, i: (0, i, 0)),
            pl.BlockSpec((1, n, 2, LANES, parts), lambda b, i: (i, 0, 0, 0, 0)),
        ],
        out_specs=[
            pl.BlockSpec((1, n, 2, tm, LANES), lambda b, i: (b, 0, 0, i, 0)),
            pl.BlockSpec((1, n, 2, LANES, tm), lambda b, i: (b, 0, 0, 0, i)),
            pl.BlockSpec((1, N_PAIRS, 2, HEAD_DIM + ONES_ROWS, tm), lambda b, i: (b, 0, 0, 0, i)),
            pl.BlockSpec((1, DIFF_HEADS, dv + ONES_ROWS, tm), lambda b, i: (b, 0, 0, i)),
            pl.BlockSpec((1, 1, N_PAIRS, parts, LANES), lambda b, i: (b, i, 0, 0, 0)),
        ],
        out_shape=[
            jax.ShapeDtypeStruct((B, n, 2, S, LANES), BF16),
            jax.ShapeDtypeStruct((B, n, 2, LANES, S), BF16),
            jax.ShapeDtypeStruct((B, N_PAIRS, 2, HEAD_DIM + ONES_ROWS, S), BF16),
            jax.ShapeDtypeStruct((B, DIFF_HEADS, dv + ONES_ROWS, S), BF16),
            jax.ShapeDtypeStruct((B, S // tm, N_PAIRS, parts, LANES), F32),
        ],
        compiler_params=pltpu.CompilerParams(
            dimension_semantics=("arbitrary", "arbitrary"), vmem_limit_bytes=VMEM_LIMIT),
        name="in_proj",
    )(x, g[:, None], w_in, kx, qx)


def _causal_tile():
    r = np.arange(BLK)
    return np.where(r[:, None] <= r[None, :], 0.0, NEG).astype(np.float32)


_CAUSAL_SPEC = pl.BlockSpec((BLK, BLK), lambda b, h: (0, 0))


def _scores_t(qa_tile, ka_ref, causal_ref, s_ref, m_ref, item_id, par, slot0, j):
    m8 = None
    for t, n in enumerate(range(j, -1, -1)):
        s = _dot(ka_ref[n * BLK:(n + 1) * BLK, :], qa_tile[...])
        if n == j:
            s = s + causal_ref[...]
        s_ref[par, slot0 + t] = s
        t8 = jnp.max(s.reshape(BLK // F32_SUBLANES, F32_SUBLANES, BLK), axis=0)
        m8 = t8 if m8 is None else jnp.maximum(m8, t8)
        yield
    m_ref[item_id] = jnp.broadcast_to(jnp.max(m8, axis=0, keepdims=True), (F32_SUBLANES, BLK))


def _weighted_values_t(s_ref, m_ref, acc_ref, item_id, par, slot0, vt_ref, j):
    acc = None
    m = m_ref[item_id][:1]
    for t, n in enumerate(range(j, -1, -1)):
        p = jnp.exp2(s_ref[par, slot0 + t] - m).astype(BF16)
        d = _dot(vt_ref[:, n * BLK:(n + 1) * BLK], p)
        acc = d if acc is None else acc + d
        yield
    acc_ref[item_id] = acc


def _rounds(nb):
    tail = min(3, nb)
    return [(j,) for j in range(nb - 1, tail - 1, -1)] + [tuple(range(tail - 1, -1, -1))]


def _round_tiles(nb):
    return max(sum(2 * (j + 1) for j in js) for js in _rounds(nb))


def _pipelined(nb, units, first, second, last, prepare=None):
    one = jnp.minimum(pl.program_id(0) + 1, 1)
    rounds = []
    for js in _rounds(nb):
        items = []
        for u in range(units):
            slot = 0
            for j in js:
                for half in range(2):
                    items.append(((u, j, half), slot))
                    slot += j + 1
        rounds.append(items)
    if prepare is not None:
        for item, _slot in rounds[0]:
            prepare(item)
    for r in range(len(rounds) + 2):
        def round_body(_, carry, r=r):
            if 0 <= r - 2 < len(rounds):
                for item, _slot in rounds[r - 2]:
                    last(item)
            if prepare is not None and r + 1 < len(rounds):
                for item, _slot in rounds[r + 1]:
                    prepare(item)
            gens = []
            if r < len(rounds):
                gens += [first(item, r % 2, slot0) for item, slot0 in rounds[r]]
            if 0 <= r - 1 < len(rounds):
                gens += [second(item, (r - 1) % 2, slot0) for item, slot0 in rounds[r - 1]]
            while gens:
                for g in list(gens):
                    if next(g, StopIteration) is StopIteration:
                        gens.remove(g)
            return carry

        lax.fori_loop(0, one, round_body, 0)


def _attn_scratch(S, vt_rows):
    nb = S // BLK
    return [
        pltpu.VMEM((UNITS, 2, _round_tiles(nb), BLK, BLK), F32),
        pltpu.VMEM((UNITS, 2 * nb, F32_SUBLANES, BLK), F32),
        pltpu.VMEM((UNITS, 2 * nb, vt_rows, BLK), F32),
    ]


def _tile_spec(shape, off):
    return pl.BlockSpec((UNITS, 1) + shape, lambda b, h: (b, off + h) + (0,) * len(shape))


def _moba_kernel(qa_in, ka_in, vt_in, ksum_in, causal_ref, o_ref, s_ref, m_ref, acc_ref, qa_ref,
                 g_ref):
    S = ka_in.shape[3]
    nb = S // BLK
    lane = lax.broadcasted_iota(jnp.int32, (1, LANES), 1)
    row8 = lax.broadcasted_iota(jnp.int32, (nb, BLK), 0)

    for u in range(UNITS):
        kmean = jnp.concatenate([ksum_in[u, i, 0] for i in range(ksum_in.shape[1])], axis=0)
        kmean = kmean * (1.0 / BLK)
        for hh in range(2):
            in_head = (lane >= hh * HEAD_DIM) & (lane < (hh + 1) * HEAD_DIM)
            km = jnp.where(in_head, kmean, 0.0)
            km_hi = km.astype(BF16)
            km_lo = (km - km_hi.astype(F32)).astype(BF16)
            g2 = _dot(jnp.concatenate([km_hi, km_lo], axis=0), qa_in[u, 0, hh])
            g_ref[u, hh] = g2[:nb] + g2[nb:]

    n_sel = max(1, min(MOBA_TOPK, nb - 1))

    def prepare(item):
        u, j, hh = item
        cols = slice(j * BLK, (j + 1) * BLK)
        qa_ref[u, 2 * j + hh] = qa_in[u, 0, hh, :, cols]
        if j > n_sel:
            gj = g_ref[u, hh, :, cols]
            cnt = jnp.zeros((nb, BLK), F32)
            for m_ in range(j):
                gm = gj[m_:m_ + 1, :]
                beats = (gm > gj) | ((gm == gj) & (m_ < row8))
                cnt = cnt + beats.astype(F32)
            pen = jnp.where((row8 >= j) | (cnt < n_sel), 0.0, NEG)
            off = HEAD_DIM * (1 - hh) + EXTRA_PEN
            pad = jnp.zeros((BF16_SUBLANES - nb, BLK), F32)
            qa_ref[u, 2 * j + hh, off:off + BF16_SUBLANES, :] = (
                jnp.concatenate([pen, pad], axis=0).astype(BF16))

    def scores(item, par, slot0):
        u, j, hh = item
        return _scores_t(qa_ref.at[u, 2 * j + hh], ka_in.at[u, 0, hh], causal_ref, s_ref.at[u],
                         m_ref.at[u], 2 * j + hh, par, slot0, j)

    def values(item, par, slot0):
        u, j, hh = item
        return _weighted_values_t(s_ref.at[u], m_ref.at[u], acc_ref.at[u], 2 * j + hh, par,
                                  slot0, vt_in.at[u, 0, hh], j)

    def finish(item):
        u, j, hh = item
        if hh == 1:
            halves = []
            for h2 in range(2):
                acc = acc_ref[u, 2 * j + h2]
                halves.append(acc[:HEAD_DIM] / acc[HEAD_DIM:HEAD_DIM + 1])
            o_t = jnp.concatenate(halves, axis=0)
            o_ref[u, 0, :, j * BLK:(j + 1) * BLK] = o_t.astype(BF16)

    _pipelined(nb, UNITS, scores, values, finish, prepare)


def _moba(ka, qa, vt_m, ksum):
    B, _, _, S, _ = ka.shape
    nb = S // BLK
    assert nb <= BF16_SUBLANES
    assert B % UNITS == 0
    rows = HEAD_DIM + ONES_ROWS
    return pl.pallas_call(
        _moba_kernel,
        grid=(B // UNITS, N_PAIRS),
        in_specs=[
            _tile_spec((2, LANES, S), 0), _tile_spec((2, S, LANES), 0),
            _tile_spec((2, rows, S), 0),
            pl.BlockSpec((UNITS, ksum.shape[1], 1) + ksum.shape[3:],
                         lambda b, h: (b, 0, h, 0, 0)),
            _CAUSAL_SPEC,
        ],
        out_specs=pl.BlockSpec((UNITS, 1, LANES, S), lambda b, h: (b, h, 0, 0)),
        out_shape=jax.ShapeDtypeStruct((B, N_PAIRS, LANES, S), BF16),
        scratch_shapes=_attn_scratch(S, rows) + [
            pltpu.VMEM((UNITS, 2 * nb, LANES, BLK), BF16),
            pltpu.VMEM((UNITS, 2, nb, S), F32),
        ],
        compiler_params=pltpu.CompilerParams(
            dimension_semantics=("arbitrary", "arbitrary"), vmem_limit_bytes=VMEM_LIMIT),
        name="moba",
    )(qa, ka, vt_m, ksum, jnp.asarray(_causal_tile()))


def _diff_kernel(qa_in, ka_in, vt_in, causal_ref, lq1_ref, lk1_ref, lq2_ref, lk2_ref, g_ref,
                 o_ref, s_ref, m_ref, acc_ref, *, lam_init):
    S = ka_in.shape[3]
    nb = S // BLK
    dv = 2 * DIFF_HEAD_DIM

    def scores(item, par, slot0):
        u, j, mp = item
        return _scores_t(qa_in.at[u, 0, mp, :, pl.ds(j * BLK, BLK)], ka_in.at[u, 0, mp],
                         causal_ref, s_ref.at[u], m_ref.at[u], 2 * j + mp, par, slot0, j)

    def values(item, par, slot0):
        u, j, mp = item
        return _weighted_values_t(s_ref.at[u], m_ref.at[u], acc_ref.at[u], 2 * j + mp, par,
                                  slot0, vt_in.at[u, 0], j)

    def finish(item):
        u, j, mp = item
        if mp == 1:
            lam = (jnp.exp(jnp.sum(lq1_ref[...] * lk1_ref[...], axis=-1, keepdims=True))
                   - jnp.exp(jnp.sum(lq2_ref[...] * lk2_ref[...], axis=-1, keepdims=True))
                   + lam_init)
            outs = []
            for m2 in range(2):
                acc = acc_ref[u, 2 * j + m2]
                outs.append(acc[:dv] / acc[dv:dv + 1])
            a = outs[0] - lam * outs[1]
            inv = lax.rsqrt(jnp.mean(a * a, axis=0, keepdims=True) + SUBLN_EPS)
            y = a * inv * g_ref[...] * (1.0 - lam_init)
            o_ref[u, 0, :, j * BLK:(j + 1) * BLK] = y.astype(BF16)

    _pipelined(nb, UNITS, scores, values, finish)


def _diff(ka, qa, vt_d, lq1, lk1, lq2, lk2, subln_g, lam_init):
    B, _, _, S, _ = ka.shape
    assert B % UNITS == 0
    H = DIFF_HEADS
    vec = lambda n: pl.BlockSpec((1, n), lambda b, h: (0, 0))
    d = DIFF_HEAD_DIM
    rows = 2 * d + ONES_ROWS
    return pl.pallas_call(
        functools.partial(_diff_kernel, lam_init=lam_init),
        grid=(B // UNITS, H),
        in_specs=[
            _tile_spec((2, LANES, S), N_PAIRS), _tile_spec((2, S, LANES), N_PAIRS),
            _tile_spec((rows, S), 0), _CAUSAL_SPEC,
            vec(d), vec(d), vec(d), vec(d),
            pl.BlockSpec((2 * d, 1), lambda b, h: (0, 0)),
        ],
        out_specs=pl.BlockSpec((UNITS, 1, LANES, S), lambda b, h: (b, h, 0, 0)),
        out_shape=jax.ShapeDtypeStruct((B, H, LANES, S), BF16),
        scratch_shapes=_attn_scratch(S, rows),
        compiler_params=pltpu.CompilerParams(
            dimension_semantics=("arbitrary", "arbitrary"), vmem_limit_bytes=VMEM_LIMIT),
        name="diff_attn",
    )(qa, ka, vt_d, jnp.asarray(_causal_tile()), lq1.reshape(1, d), lk1.reshape(1, d),
      lq2.reshape(1, d), lk2.reshape(1, d), subln_g.reshape(2 * d, 1))


def _merge_kernel(x_ref, ya_ref, yb_ref, gpre_ref, wga_ref, wgb_ref, wa_ref, wb_ref, wo_ref,
                  gpost_ref, f0_ref, f1_ref, f2_ref, o_ref, c0_ref, c1_ref, c2_ref, *, parts):
    @pl.when(pl.program_id(1) == 0)
    def _():
        for src, dst in ((f0_ref, c0_ref), (f1_ref, c1_ref), (f2_ref, c2_ref)):
            dst[...] = src[...].astype(BF16)

    _, tm, D = x_ref.shape
    rows = [slice(p * tm // parts, (p + 1) * tm // parts) for p in range(parts)]
    hs = [_rms(x_ref[0, r, :], gpre_ref[0], NORM_EPS).astype(BF16) for r in rows]
    for r, h in zip(rows, hs):
        ga = _dot(h, wga_ref[0])
        a = _dot_tn(ya_ref[0, :, r], wa_ref[0])
        merged = jax.nn.sigmoid(ga) * a
        gb = _dot(h, wgb_ref[0])
        b = _dot_tn(yb_ref[0, :, r], wb_ref[0])
        merged = merged + jax.nn.sigmoid(gb) * b
        o = _dot(merged.astype(BF16), wo_ref[0])
        o_ref[0, r, :] = x_ref[0, r, :] + _rms(o, gpost_ref[0], NORM_EPS)


def _merge(x, ya_t, yb_t, gpre, w_in, wa, wb, wo, gpost, ffn_ws, l, tm=1024):
    B, S, D = x.shape
    gate_col = QKV_COLS // D
    slab_in, slab_out, cast_shapes = [], [], []
    for w in ffn_ws:
        _, rows, cols = w.shape
        assert rows % (B * BF16_SUBLANES) == 0
        slab_in.append(pl.BlockSpec((1, rows // B, cols), lambda b, i: (l, b, 0)))
        slab_out.append(pl.BlockSpec((1, rows // B, cols), lambda b, i: (0, b, 0)))
        cast_shapes.append(jax.ShapeDtypeStruct((1, rows, cols), BF16))
    return pl.pallas_call(
        functools.partial(_merge_kernel, parts=4),
        grid=(B, S // tm),
        in_specs=[
            pl.BlockSpec((1, tm, D), lambda b, i: (b, i, 0)),
            pl.BlockSpec((1, MOBA_WIDTH, tm), lambda b, i: (b, 0, i)),
            pl.BlockSpec((1, DIFF_WIDTH, tm), lambda b, i: (b, 0, i)),
            _layer_spec((1, D), l),
            _layer_spec((D, D), l, gate_col),
            _layer_spec((D, D), l, gate_col + 1),
            _layer_spec((MOBA_WIDTH, D), l),
            _layer_spec((DIFF_WIDTH, D), l),
            _layer_spec((D, D), l),
            _layer_spec((1, D), l),
        ] + slab_in,
        out_specs=[pl.BlockSpec((1, tm, D), lambda b, i: (b, i, 0))] + slab_out,
        out_shape=[jax.ShapeDtypeStruct((B, S, D), F32)] + cast_shapes,
        compiler_params=pltpu.CompilerParams(
            dimension_semantics=("arbitrary", "arbitrary"), vmem_limit_bytes=VMEM_LIMIT),
        name="merge",
    )(x, ya_t, yb_t, gpre[:, None], w_in, w_in, wa, wb, wo, gpost[:, None], *ffn_ws)


def _ffn_kernel(x_ref, gpre_ref, wg_ref, wu_ref, wd_ref, gpost_ref, o_ref, a_ref, *, chunk, parts):
    tm = x_ref.shape[0]
    rows = [slice(p * tm // parts, (p + 1) * tm // parts) for p in range(parts)]
    hs = [_rms(x_ref[r, :], gpre_ref[0], NORM_EPS).astype(BF16) for r in rows]
    for c in range(D_FF // chunk):
        cols = slice(c * chunk, (c + 1) * chunk)
        for r, h in zip(rows, hs):
            g = _dot(h, wg_ref[0, :, cols])
            u = _dot(h, wu_ref[0, :, cols])
            a_ref[r, cols] = (jax.nn.silu(g) * u).astype(BF16)
    for r in rows:
        f = _dot(a_ref[r, :], wd_ref[0])
        o_ref[r, :] = x_ref[r, :] + _rms(f, gpost_ref[0], NORM_EPS)


def _ffn(x2, gpre, wg, wu, wd, gpost, l, tm=1024):
    T, D = x2.shape
    return pl.pallas_call(
        functools.partial(_ffn_kernel, chunk=D_FF // 2, parts=4),
        grid=(T // tm,),
        in_specs=[
            pl.BlockSpec((tm, D), lambda i: (i, 0)),
            _layer_spec((1, D), l),
            _layer_spec((D, D_FF), 0),
            _layer_spec((D, D_FF), 0),
            _layer_spec((D_FF, D), 0),
            _layer_spec((1, D), l),
        ],
        out_specs=pl.BlockSpec((tm, D), lambda i: (i, 0)),
        out_shape=jax.ShapeDtypeStruct((T, D), F32),
        scratch_shapes=[pltpu.VMEM((tm, D_FF), BF16)],
        compiler_params=pltpu.CompilerParams(
            dimension_semantics=("arbitrary",), vmem_limit_bytes=VMEM_LIMIT),
        name="ffn",
    )(x2, gpre[:, None], wg, wu, wd, gpost[:, None])


def kernel(x, norm_mix_pre_g, w_in, w_branch_a, w_branch_b, lam_q1, lam_k1, lam_q2, lam_k2,
           diff_subln_g, w_out, norm_mix_post_g, norm_ffn_pre_g, w_gate, w_up, w_down,
           norm_ffn_post_g):
    B, S, D = x.shape
    assert D == D_MODEL and S % BLK == 0
    depth = w_in.shape[0]
    for l in range(depth):
        lam_init = 0.8 - 0.6 * math.exp(-0.3 * l)
        ka, qa, vt_m, vt_d, ksum = _in_proj(x, norm_mix_pre_g, w_in, l)
        ya = _moba(ka, qa, vt_m, ksum)
        yb = _diff(ka, qa, vt_d, lam_q1[l], lam_k1[l], lam_q2[l], lam_k2[l],
                   diff_subln_g[l], lam_init)
        x1, wg16, wu16, wd16 = _merge(
            x, ya.reshape(B, MOBA_WIDTH, S), yb.reshape(B, DIFF_WIDTH, S), norm_mix_pre_g, w_in,
            w_branch_a, w_branch_b, w_out, norm_mix_post_g, (w_gate, w_up, w_down), l)
        x2 = _ffn(x1.reshape(B * S, D), norm_ffn_pre_g, wg16, wu16, wd16, norm_ffn_post_g, l)
        x = x2.reshape(B, S, D)
    return x
```

```python
import functools
import math

import numpy as np
import jax
import jax.numpy as jnp
from jax import lax
from jax.experimental import pallas as pl
from jax.experimental.pallas import tpu as pltpu

D_MODEL = 1024
MOBA_HEADS = 8
MOBA_TOPK = 3
MOBA_WIDTH = 512
DIFF_HEADS = 4
DIFF_HEAD_DIM = 64
DIFF_WIDTH = 512
HEAD_DIM = 64
QKV_COLS = 3 * MOBA_WIDTH + 3 * DIFF_WIDTH
D_FF = 2816
NORM_EPS = 1e-6
SUBLN_EPS = 1e-5
ALIBI_MAX_BIAS = 8.0
NEG = -1e30
LOG2E = math.log2(math.e)
Q_SCALE = HEAD_DIM ** -0.5 * LOG2E

LANES = 128
BLK = 256
N_COL_TILES = QKV_COLS // LANES
N_QK_TILES = N_COL_TILES // 3
N_PAIRS = MOBA_HEADS // 2
GROUP_TILES = N_PAIRS
assert DIFF_HEADS == GROUP_TILES and N_QK_TILES == 2 * GROUP_TILES
F32_SUBLANES = 8
BF16_SUBLANES = 16
ONES_ROWS = BF16_SUBLANES
EXTRA_PEN = BF16_SUBLANES
UNITS = 2
VMEM_LIMIT = 56 * 1024 * 1024

BF16 = jnp.bfloat16
F32 = jnp.float32


def _alibi_slopes():
    n = MOBA_HEADS + DIFF_HEADS
    slopes = 2.0 ** (-ALIBI_MAX_BIAS * np.arange(1, n + 1) / n)
    stride = n // DIFF_HEADS
    diff_idx = np.arange(DIFF_HEADS) * stride + (stride - 1)
    moba_idx = np.setdiff1d(np.arange(n), diff_idx)
    return (np.asarray(slopes[moba_idx], np.float32), np.asarray(slopes[diff_idx], np.float32))


def _split3(v):
    parts, rem = [], float(v)
    for _ in range(3):
        p = float(np.asarray(rem, np.float32).astype(BF16).astype(np.float64))
        parts.append(p)
        rem -= p
    return parts


def _attn_tables(slopes, S):
    nb = S // BLK
    H = len(slopes)
    qx = np.zeros((H, 2, nb, LANES), np.float32)
    kx = np.zeros((2, S, LANES), np.float32)
    pos = np.arange(S)
    for half in range(2):
        e0 = HEAD_DIM * (1 - half)
        for c in range(3):
            kx[half, :, e0 + c] = pos % BLK
            kx[half, :, e0 + 3 + c] = (pos // BLK) * BLK
            kx[half, :, e0 + 6 + c] = 1.0
        for n in range(nb):
            kx[half, n * BLK:(n + 1) * BLK, e0 + EXTRA_PEN + n] = 1.0
        for h in range(H):
            s = float(np.float32(slopes[h])) * LOG2E
            sp = _split3(s)
            for j in range(nb):
                cp = _split3(-s * j * BLK)
                for c in range(3):
                    qx[h, half, j, e0 + c] = sp[c]
                    qx[h, half, j, e0 + 3 + c] = sp[c]
                    qx[h, half, j, e0 + 6 + c] = cp[c]
    return qx.transpose(0, 1, 3, 2), kx


def _query_extras(S, per_step):
    nb = S // BLK
    moba_slopes, diff_slopes = _alibi_slopes()
    qx_m, _ = _attn_tables(moba_slopes, S)
    qx_d, _ = _attn_tables(diff_slopes, S)
    qx_m = qx_m.reshape(N_PAIRS, 2, 2, LANES, nb)[:, np.arange(2), np.arange(2)]
    qx = np.concatenate([qx_m, qx_d], axis=0)
    qx = qx.reshape(N_QK_TILES, 2, LANES, nb // per_step, per_step)
    return np.ascontiguousarray(qx.transpose(3, 0, 1, 2, 4))


def _dot(a, b):
    return jnp.dot(a, b, preferred_element_type=F32)


def _dot_tn(a_t, b):
    return lax.dot_general(a_t, b, (((0,), (0,)), ((), ())), preferred_element_type=F32)


def _rms(x, g, eps):
    return x * lax.rsqrt(jnp.mean(x * x, axis=-1, keepdims=True) + eps) * g


def _layer_spec(block, l, col_block=0):
    index = (l,) + (0,) * (len(block) - 1) + (col_block,)
    return pl.BlockSpec((1,) + tuple(block), lambda *_: index, pipeline_mode=pl.Buffered(1))


def _in_proj_kernel(x_ref, g_ref, w_ref, kx_ref, qx_ref, ka_out, qa_out, vtm_out, vtd_out,
                    ksum_out, *, parts):
    tm = x_ref.shape[1]
    part = tm // parts
    assert part == BLK
    chunk = GROUP_TILES * LANES
    lane = lax.broadcasted_iota(jnp.int32, (1, LANES), 1)
    row = lax.broadcasted_iota(jnp.int32, (LANES, 1), 0)
    in_half = [(lane >= h * HEAD_DIM) & (lane < (h + 1) * HEAD_DIM) for h in range(2)]
    half_rows = [(row >= h * HEAD_DIM) & (row < (h + 1) * HEAD_DIM) for h in range(2)]
    ones = jnp.ones((ONES_ROWS, part), BF16)
    rows = [slice(p * part, (p + 1) * part) for p in range(parts)]
    hs = [_rms(x_ref[0, r, :], g_ref[0], NORM_EPS).astype(BF16) for r in rows]
    def emit(p, r, c, acc):
        kind, mixer = ("q", "k", "v")[c % 3], c // 3
        for t in range(GROUP_TILES):
            tile = GROUP_TILES * mixer + t
            val = acc[:, t * LANES:(t + 1) * LANES]
            if kind == "k":
                if mixer == 0:
                    ksum_out[0, 0, t, p:p + 1, :] = jnp.sum(val, axis=0, keepdims=True)
                val = val.astype(BF16)
                for half in range(2):
                    ka_out[0, tile, half, r, :] = jnp.where(
                        in_half[half], val, kx_ref[half, r, :])
            elif kind == "q":
                val_t = (val * Q_SCALE).astype(BF16).T
                for half in range(2):
                    ext = jnp.broadcast_to(qx_ref[0, tile, half][:, p:p + 1], (LANES, part))
                    qa_out[0, tile, half, :, r] = jnp.where(
                        half_rows[half], val_t, ext.astype(BF16))
            else:
                val_t = val.astype(BF16).T
                if mixer == 0:
                    for hh in range(2):
                        vtm_out[0, t, hh, :, r] = jnp.concatenate(
                            [val_t[hh * HEAD_DIM:(hh + 1) * HEAD_DIM], ones], axis=0)
                else:
                    vtd_out[0, t, :, r] = jnp.concatenate([val_t, ones], axis=0)

    for p, (r, h) in enumerate(zip(rows, hs)):
        for c in range(QKV_COLS // chunk):
            emit(p, r, c, _dot(h, w_ref[0, :, c * chunk:(c + 1) * chunk]))


def _in_proj(x, g, w_in, l, tm=1024):
    B, S, D = x.shape
    parts = tm // BLK
    kx = jnp.asarray(_attn_tables(_alibi_slopes()[0], S)[1], BF16)
    qx = jnp.asarray(_query_extras(S, parts))
    n, dv = N_QK_TILES, 2 * DIFF_HEAD_DIM
    return pl.pallas_call(
        functools.partial(_in_proj_kernel, parts=parts),
        grid=(B, S // tm),
        in_specs=[
            pl.BlockSpec((1, tm, D), lambda b, i: (b, i, 0)),
            _layer_spec((1, D), l),
            _layer_spec((D, QKV_COLS), l),
            pl.BlockSpec((2, tm, LANES), lambda b, i: (0, i, 0)),
            pl.BlockSpec((1, n, 2, LANES, parts), lambda b, i: (i, 0, 0, 0, 0)),
        ],
        out_specs=[
            pl.BlockSpec((1, n, 2, tm, LANES), lambda b, i: (b, 0, 0, i, 0)),
            pl.BlockSpec((1, n, 2, LANES, tm), lambda b, i: (b, 0, 0, 0, i)),
            pl.BlockSpec((1, N_PAIRS, 2, HEAD_DIM + ONES_ROWS, tm), lambda b, i: (b, 0, 0, 0, i)),
            pl.BlockSpec((1, DIFF_HEADS, dv + ONES_ROWS, tm), lambda b, i: (b, 0, 0, i)),
            pl.BlockSpec((1, 1, N_PAIRS, parts, LANES), lambda b, i: (b, i, 0, 0, 0)),
        ],
        out_shape=[
            jax.ShapeDtypeStruct((B, n, 2, S, LANES), BF16),
            jax.ShapeDtypeStruct((B, n, 2, LANES, S), BF16),
            jax.ShapeDtypeStruct((B, N_PAIRS, 2, HEAD_DIM + ONES_ROWS, S), BF16),
            jax.ShapeDtypeStruct((B, DIFF_HEADS, dv + ONES_ROWS, S), BF16),
            jax.ShapeDtypeStruct((B, S // tm, N_PAIRS, parts, LANES), F32),
        ],
        compiler_params=pltpu.CompilerParams(
            dimension_semantics=("arbitrary", "arbitrary"), vmem_limit_bytes=VMEM_LIMIT),
        name="in_proj",
    )(x, g[:, None], w_in, kx, qx)


def _causal_tile():
    r = np.arange(BLK)
    return np.where(r[:, None] <= r[None, :], 0.0, NEG).astype(np.float32)


_CAUSAL_SPEC = pl.BlockSpec((BLK, BLK), lambda b, h: (0, 0))


def _scores_t(qa_tile, ka_ref, causal_ref, s_ref, m_ref, item_id, par, slot0, j):
    m8 = None
    for t, n in enumerate(range(j, -1, -1)):
        s = _dot(ka_ref[n * BLK:(n + 1) * BLK, :], qa_tile[...])
        if n == j:
            s = s + causal_ref[...]
        s_ref[par, slot0 + t] = s
        t8 = jnp.max(s.reshape(BLK // F32_SUBLANES, F32_SUBLANES, BLK), axis=0)
        m8 = t8 if m8 is None else jnp.maximum(m8, t8)
        yield
    m_ref[item_id] = jnp.broadcast_to(jnp.max(m8, axis=0, keepdims=True), (F32_SUBLANES, BLK))


def _weighted_values_t(s_ref, m_ref, acc_ref, item_id, par, slot0, vt_ref, j):
    acc = None
    m = m_ref[item_id][:1]
    for t, n in enumerate(range(j, -1, -1)):
        p = jnp.exp2(s_ref[par, slot0 + t] - m).astype(BF16)
        d = _dot(vt_ref[:, n * BLK:(n + 1) * BLK], p)
        acc = d if acc is None else acc + d
        yield
    acc_ref[item_id] = acc


def _rounds(nb):
    tail = min(3, nb)
    return [(j,) for j in range(nb - 1, tail - 1, -1)] + [tuple(range(tail - 1, -1, -1))]


def _round_tiles(nb):
    return max(sum(2 * (j + 1) for j in js) for js in _rounds(nb))


def _pipelined(nb, units, first, second, last, prepare=None):
    one = jnp.minimum(pl.program_id(0) + 1, 1)
    rounds = []
    for js in _rounds(nb):
        items = []
        for u in range(units):
            slot = 0
            for j in js:
                for half in range(2):
                    items.append(((u, j, half), slot))
                    slot += j + 1
        rounds.append(items)
    if prepare is not None:
        for item, _slot in rounds[0]:
            prepare(item)
    for r in range(len(rounds) + 2):
        def round_body(_, carry, r=r):
            if 0 <= r - 2 < len(rounds):
                for item, _slot in rounds[r - 2]:
                    last(item)
            if prepare is not None and r + 1 < len(rounds):
                for item, _slot in rounds[r + 1]:
                    prepare(item)
            gens = []
            if r < len(rounds):
                gens += [first(item, r % 2, slot0) for item, slot0 in rounds[r]]
            if 0 <= r - 1 < len(rounds):
                gens += [second(item, (r - 1) % 2, slot0) for item, slot0 in rounds[r - 1]]
            while gens:
                for g in list(gens):
                    if next(g, StopIteration) is StopIteration:
                        gens.remove(g)
            return carry

        lax.fori_loop(0, one, round_body, 0)


def _attn_scratch(S, vt_rows):
    nb = S // BLK
    return [
        pltpu.VMEM((UNITS, 2, _round_tiles(nb), BLK, BLK), F32),
        pltpu.VMEM((UNITS, 2 * nb, F32_SUBLANES, BLK), F32),
        pltpu.VMEM((UNITS, 2 * nb, vt_rows, BLK), F32),
    ]


def _tile_spec(shape, off):
    return pl.BlockSpec((UNITS, 1) + shape, lambda b, h: (b, off + h) + (0,) * len(shape))


def _moba_kernel(qa_in, ka_in, vt_in, ksum_in, causal_ref, o_ref, s_ref, m_ref, acc_ref, qa_ref,
                 g_ref):
    S = ka_in.shape[3]
    nb = S // BLK
    lane = lax.broadcasted_iota(jnp.int32, (1, LANES), 1)
    row8 = lax.broadcasted_iota(jnp.int32, (nb, BLK), 0)

    for u in range(UNITS):
        kmean = jnp.concatenate([ksum_in[u, i, 0] for i in range(ksum_in.shape[1])], axis=0)
        kmean = kmean * (1.0 / BLK)
        for hh in range(2):
            in_head = (lane >= hh * HEAD_DIM) & (lane < (hh + 1) * HEAD_DIM)
            km = jnp.where(in_head, kmean, 0.0)
            km_hi = km.astype(BF16)
            km_lo = (km - km_hi.astype(F32)).astype(BF16)
            g2 = _dot(jnp.concatenate([km_hi, km_lo], axis=0), qa_in[u, 0, hh])
            g_ref[u, hh] = g2[:nb] + g2[nb:]

    n_sel = max(1, min(MOBA_TOPK, nb - 1))

    def prepare(item):
        u, j, hh = item
        cols = slice(j * BLK, (j + 1) * BLK)
        qa_ref[u, 2 * j + hh] = qa_in[u, 0, hh, :, cols]
        if j > n_sel:
            gj = g_ref[u, hh, :, cols]
            cnt = jnp.zeros((nb, BLK), F32)
            for m_ in range(j):
                gm = gj[m_:m_ + 1, :]
                beats = (gm > gj) | ((gm == gj) & (m_ < row8))
                cnt = cnt + beats.astype(F32)
            pen = jnp.where((row8 >= j) | (cnt < n_sel), 0.0, NEG)
            off = HEAD_DIM * (1 - hh) + EXTRA_PEN
            pad = jnp.zeros((BF16_SUBLANES - nb, BLK), F32)
            qa_ref[u, 2 * j + hh, off:off + BF16_SUBLANES, :] = (
                jnp.concatenate([pen, pad], axis=0).astype(BF16))

    def scores(item, par, slot0):
        u, j, hh = item
        return _scores_t(qa_ref.at[u, 2 * j + hh], ka_in.at[u, 0, hh], causal_ref, s_ref.at[u],
                         m_ref.at[u], 2 * j + hh, par, slot0, j)

    def values(item, par, slot0):
        u, j, hh = item
        return _weighted_values_t(s_ref.at[u], m_ref.at[u], acc_ref.at[u], 2 * j + hh, par,
                                  slot0, vt_in.at[u, 0, hh], j)

    def finish(item):
        u, j, hh = item
        if hh == 1:
            halves = []
            for h2 in range(2):
                acc = acc_ref[u, 2 * j + h2]
                halves.append(acc[:HEAD_DIM] / acc[HEAD_DIM:HEAD_DIM + 1])
            o_t = jnp.concatenate(halves, axis=0)
            o_ref[u, 0, :, j * BLK:(j + 1) * BLK] = o_t.astype(BF16)

    _pipelined(nb, UNITS, scores, values, finish, prepare)


def _moba(ka, qa, vt_m, ksum):
    B, _, _, S, _ = ka.shape
    nb = S // BLK
    assert nb <= BF16_SUBLANES
    assert B % UNITS == 0
    rows = HEAD_DIM + ONES_ROWS
    return pl.pallas_call(
        _moba_kernel,
        grid=(B // UNITS, N_PAIRS),
        in_specs=[
            _tile_spec((2, LANES, S), 0), _tile_spec((2, S, LANES), 0),
            _tile_spec((2, rows, S), 0),
            pl.BlockSpec((UNITS, ksum.shape[1], 1) + ksum.shape[3:],
                         lambda b, h: (b, 0, h, 0, 0)),
            _CAUSAL_SPEC,
        ],
        out_specs=pl.BlockSpec((UNITS, 1, LANES, S), lambda b, h: (b, h, 0, 0)),
        out_shape=jax.ShapeDtypeStruct((B, N_PAIRS, LANES, S), BF16),
        scratch_shapes=_attn_scratch(S, rows) + [
            pltpu.VMEM((UNITS, 2 * nb, LANES, BLK), BF16),
            pltpu.VMEM((UNITS, 2, nb, S), F32),
        ],
        compiler_params=pltpu.CompilerParams(
            dimension_semantics=("arbitrary", "arbitrary"), vmem_limit_bytes=VMEM_LIMIT),
        name="moba",
    )(qa, ka, vt_m, ksum, jnp.asarray(_causal_tile()))


def _diff_kernel(qa_in, ka_in, vt_in, causal_ref, lq1_ref, lk1_ref, lq2_ref, lk2_ref, g_ref,
                 o_ref, s_ref, m_ref, acc_ref, *, lam_init):
    S = ka_in.shape[3]
    nb = S // BLK
    dv = 2 * DIFF_HEAD_DIM

    def scores(item, par, slot0):
        u, j, mp = item
        return _scores_t(qa_in.at[u, 0, mp, :, pl.ds(j * BLK, BLK)], ka_in.at[u, 0, mp],
                         causal_ref, s_ref.at[u], m_ref.at[u], 2 * j + mp, par, slot0, j)

    def values(item, par, slot0):
        u, j, mp = item
        return _weighted_values_t(s_ref.at[u], m_ref.at[u], acc_ref.at[u], 2 * j + mp, par,
                                  slot0, vt_in.at[u, 0], j)

    def finish(item):
        u, j, mp = item
        if mp == 1:
            lam = (jnp.exp(jnp.sum(lq1_ref[...] * lk1_ref[...], axis=-1, keepdims=True))
                   - jnp.exp(jnp.sum(lq2_ref[...] * lk2_ref[...], axis=-1, keepdims=True))
                   + lam_init)
            outs = []
            for m2 in range(2):
                acc = acc_ref[u, 2 * j + m2]
                outs.append(acc[:dv] / acc[dv:dv + 1])
            a = outs[0] - lam * outs[1]
            inv = lax.rsqrt(jnp.mean(a * a, axis=0, keepdims=True) + SUBLN_EPS)
            y = a * inv * g_ref[...] * (1.0 - lam_init)
            o_ref[u, 0, :, j * BLK:(j + 1) * BLK] = y.astype(BF16)

    _pipelined(nb, UNITS, scores, values, finish)


def _diff(ka, qa, vt_d, lq1, lk1, lq2, lk2, subln_g, lam_init):
    B, _, _, S, _ = ka.shape
    assert B % UNITS == 0
    H = DIFF_HEADS
    vec = lambda n: pl.BlockSpec((1, n), lambda b, h: (0, 0))
    d = DIFF_HEAD_DIM
    rows = 2 * d + ONES_ROWS
    return pl.pallas_call(
        functools.partial(_diff_kernel, lam_init=lam_init),
        grid=(B // UNITS, H),
        in_specs=[
            _tile_spec((2, LANES, S), N_PAIRS), _tile_spec((2, S, LANES), N_PAIRS),
            _tile_spec((rows, S), 0), _CAUSAL_SPEC,
            vec(d), vec(d), vec(d), vec(d),
            pl.BlockSpec((2 * d, 1), lambda b, h: (0, 0)),
        ],
        out_specs=pl.BlockSpec((UNITS, 1, LANES, S), lambda b, h: (b, h, 0, 0)),
        out_shape=jax.ShapeDtypeStruct((B, H, LANES, S), BF16),
        scratch_shapes=_attn_scratch(S, rows),
        compiler_params=pltpu.CompilerParams(
            dimension_semantics=("arbitrary", "arbitrary"), vmem_limit_bytes=VMEM_LIMIT),
        name="diff_attn",
    )(qa, ka, vt_d, jnp.asarray(_causal_tile()), lq1.reshape(1, d), lk1.reshape(1, d),
      lq2.reshape(1, d), lk2.reshape(1, d), subln_g.reshape(2 * d, 1))


def _merge_kernel(x_ref, ya_ref, yb_ref, gpre_ref, wga_ref, wgb_ref, wa_ref, wb_ref, wo_ref,
                  gpost_ref, f0_ref, f1_ref, f2_ref, o_ref, c0_ref, c1_ref, c2_ref, *, parts):
    @pl.when(pl.program_id(1) == 0)
    def _():
        for src, dst in ((f0_ref, c0_ref), (f1_ref, c1_ref), (f2_ref, c2_ref)):
            dst[...] = src[...].astype(BF16)

    _, tm, D = x_ref.shape
    rows = [slice(p * tm // parts, (p + 1) * tm // parts) for p in range(parts)]
    hs = [_rms(x_ref[0, r, :], gpre_ref[0], NORM_EPS).astype(BF16) for r in rows]
    def mix(r, h):
        ga = _dot(h, wga_ref[0])
        a = _dot_tn(ya_ref[0, :, r], wa_ref[0])
        merged = jax.nn.sigmoid(ga) * a
        gb = _dot(h, wgb_ref[0])
        b = _dot_tn(yb_ref[0, :, r], wb_ref[0])
        return (merged + jax.nn.sigmoid(gb) * b).astype(BF16)

    def project(r, merged):
        o = _dot(merged, wo_ref[0])
        o_ref[0, r, :] = x_ref[0, r, :] + _rms(o, gpost_ref[0], NORM_EPS)

    pending = None
    for r, h in zip(rows, hs):
        merged = mix(r, h)
        if pending is not None:
            project(*pending)
        pending = (r, merged)
    project(*pending)


def _merge(x, ya_t, yb_t, gpre, w_in, wa, wb, wo, gpost, ffn_ws, l, tm=1024):
    B, S, D = x.shape
    gate_col = QKV_COLS // D
    slab_in, slab_out, cast_shapes = [], [], []
    for w in ffn_ws:
        _, rows, cols = w.shape
        assert rows % (B * BF16_SUBLANES) == 0
        slab_in.append(pl.BlockSpec((1, rows // B, cols), lambda b, i: (l, b, 0)))
        slab_out.append(pl.BlockSpec((1, rows // B, cols), lambda b, i: (0, b, 0)))
        cast_shapes.append(jax.ShapeDtypeStruct((1, rows, cols), BF16))
    return pl.pallas_call(
        functools.partial(_merge_kernel, parts=4),
        grid=(B, S // tm),
        in_specs=[
            pl.BlockSpec((1, tm, D), lambda b, i: (b, i, 0)),
            pl.BlockSpec((1, MOBA_WIDTH, tm), lambda b, i: (b, 0, i)),
            pl.BlockSpec((1, DIFF_WIDTH, tm), lambda b, i: (b, 0, i)),
            _layer_spec((1, D), l),
            _layer_spec((D, D), l, gate_col),
            _layer_spec((D, D), l, gate_col + 1),
            _layer_spec((MOBA_WIDTH, D), l),
            _layer_spec((DIFF_WIDTH, D), l),
            _layer_spec((D, D), l),
            _layer_spec((1, D), l),
        ] + slab_in,
        out_specs=[pl.BlockSpec((1, tm, D), lambda b, i: (b, i, 0))] + slab_out,
        out_shape=[jax.ShapeDtypeStruct((B, S, D), F32)] + cast_shapes,
        compiler_params=pltpu.CompilerParams(
            dimension_semantics=("arbitrary", "arbitrary"), vmem_limit_bytes=VMEM_LIMIT),
        name="merge",
    )(x, ya_t, yb_t, gpre[:, None], w_in, w_in, wa, wb, wo, gpost[:, None], *ffn_ws)


def _ffn_kernel(x_ref, gpre_ref, wg_ref, wu_ref, wd_ref, gpost_ref, o_ref, a_ref, *, chunk, parts):
    tm = x_ref.shape[0]
    rows = [slice(p * tm // parts, (p + 1) * tm // parts) for p in range(parts)]
    hs = [_rms(x_ref[r, :], gpre_ref[0], NORM_EPS).astype(BF16) for r in rows]
    for c in range(D_FF // chunk):
        cols = slice(c * chunk, (c + 1) * chunk)
        for r, h in zip(rows, hs):
            g = _dot(h, wg_ref[0, :, cols])
            u = _dot(h, wu_ref[0, :, cols])
            a_ref[r, cols] = (jax.nn.silu(g) * u).astype(BF16)
    for r in rows:
        f = _dot(a_ref[r, :], wd_ref[0])
        o_ref[r, :] = x_ref[r, :] + _rms(f, gpost_ref[0], NORM_EPS)


def _ffn(x2, gpre, wg, wu, wd, gpost, l, tm=1024):
    T, D = x2.shape
    return pl.pallas_call(
        functools.partial(_ffn_kernel, chunk=D_FF // 2, parts=4),
        grid=(T // tm,),
        in_specs=[
            pl.BlockSpec((tm, D), lambda i: (i, 0)),
            _layer_spec((1, D), l),
            _layer_spec((D, D_FF), 0),
            _layer_spec((D, D_FF), 0),
            _layer_spec((D_FF, D), 0),
            _layer_spec((1, D), l),
        ],
        out_specs=pl.BlockSpec((tm, D), lambda i: (i, 0)),
        out_shape=jax.ShapeDtypeStruct((T, D), F32),
        scratch_shapes=[pltpu.VMEM((tm, D_FF), BF16)],
        compiler_params=pltpu.CompilerParams(
            dimension_semantics=("arbitrary",), vmem_limit_bytes=VMEM_LIMIT),
        name="ffn",
    )(x2, gpre[:, None], wg, wu, wd, gpost[:, None])


def kernel(x, norm_mix_pre_g, w_in, w_branch_a, w_branch_b, lam_q1, lam_k1, lam_q2, lam_k2,
           diff_subln_g, w_out, norm_mix_post_g, norm_ffn_pre_g, w_gate, w_up, w_down,
           norm_ffn_post_g):
    B, S, D = x.shape
    assert D == D_MODEL and S % BLK == 0
    depth = w_in.shape[0]
    for l in range(depth):
        lam_init = 0.8 - 0.6 * math.exp(-0.3 * l)
        ka, qa, vt_m, vt_d, ksum = _in_proj(x, norm_mix_pre_g, w_in, l)
        ya = _moba(ka, qa, vt_m, ksum)
        yb = _diff(ka, qa, vt_d, lam_q1[l], lam_k1[l], lam_q2[l], lam_k2[l],
                   diff_subln_g[l], lam_init)
        x1, wg16, wu16, wd16 = _merge(
            x, ya.reshape(B, MOBA_WIDTH, S), yb.reshape(B, DIFF_WIDTH, S), norm_mix_pre_g, w_in,
            w_branch_a, w_branch_b, w_out, norm_mix_post_g, (w_gate, w_up, w_down), l)
        x2 = _ffn(x1.reshape(B * S, D), norm_ffn_pre_g, wg16, wu16, wd16, norm_ffn_post_g, l)
        x = x2.reshape(B, S, D)
    return x
```
